```python
import jax, jax.numpy as jnp
from jax import lax
import numpy as np

D_MODEL = 2048
BATCH = 4
SEQ = 2048
DEPTH = 1
DEC_BATCH = 128
DEC_SEQ = 1
PAST_LEN = 16384
PAGE_SIZE = 128

N_HEADS = 32
N_KV_HEADS = 8
HEAD_DIM = D_MODEL // N_HEADS
GQA_GROUP = N_HEADS // N_KV_HEADS
WINDOW = 128
ROPE_THETA = 10000.0
ATTN_WIDTH = N_HEADS * HEAD_DIM
KV_WIDTH = N_KV_HEADS * HEAD_DIM
SSM_EXPAND = 2
D_INNER = SSM_EXPAND * D_MODEL
SSM_HEAD_DIM = 64
N_SSM_HEADS = D_INNER // SSM_HEAD_DIM
D_STATE = 128
N_SSM_GROUPS = 8
HEADS_PER_GROUP = N_SSM_HEADS // N_SSM_GROUPS
CONV_W = 4
CONV_DIM = D_INNER + 2 * N_SSM_GROUPS * D_STATE
CHUNK = 128
N_BRANCH = 2
D_FF = 256 * (-(-8 * D_MODEL // (3 * 256)))
EPS = 1e-6
SPLITS = (ATTN_WIDTH,
          ATTN_WIDTH + KV_WIDTH,
          ATTN_WIDTH + 2 * KV_WIDTH,
          ATTN_WIDTH + 2 * KV_WIDTH + D_INNER,
          ATTN_WIDTH + 2 * KV_WIDTH + D_INNER + CONV_DIM,
          ATTN_WIDTH + 2 * KV_WIDTH + D_INNER + CONV_DIM + N_SSM_HEADS)
IN_PROJ_DIM = ATTN_WIDTH + 2 * KV_WIDTH + D_INNER + CONV_DIM + N_SSM_HEADS + N_BRANCH * D_MODEL

kernel_name = 'hybrid_ssd_swa_sink_step'


def rms_norm(x, w):
    xf = x.astype(jnp.float32)
    y = xf * lax.rsqrt(jnp.mean(xf * xf, axis=-1, keepdims=True) + EPS)
    return (y * w.astype(jnp.float32)).astype(x.dtype)


def gated_group_rms_norm(y, z, w):
    g = (y.astype(jnp.float32) * jax.nn.silu(z.astype(jnp.float32)))
    g = g.reshape(g.shape[:-1] + (N_SSM_GROUPS, D_INNER // N_SSM_GROUPS))
    g = g * lax.rsqrt(jnp.mean(g * g, axis=-1, keepdims=True) + EPS)
    return (g.reshape(y.shape) * w.astype(jnp.float32)).astype(y.dtype)


def rope(x, pos):
    half = HEAD_DIM // 2
    inv = ROPE_THETA ** (-jnp.arange(half, dtype=jnp.float32) / half)
    ang = pos.astype(jnp.float32)[:, None] * inv[None, :]
    cos = jnp.cos(ang)[None, :, None, :]
    sin = jnp.sin(ang)[None, :, None, :]
    xf = x.astype(jnp.float32)
    x1, x2 = xf[..., :half], xf[..., half:]
    return jnp.concatenate([x1 * cos - x2 * sin, x2 * cos + x1 * sin], axis=-1).astype(x.dtype)


def sink_softmax(s, mask, sink):
    s = jnp.where(mask, s, -jnp.inf)
    sk = jnp.broadcast_to(sink.astype(jnp.float32).reshape(N_KV_HEADS, GQA_GROUP, 1, 1), s.shape[:-1] + (1,))
    p = jax.nn.softmax(jnp.concatenate([s, sk], axis=-1), axis=-1)
    return p[..., :-1]


def window_attn_prompt(q, k, v, sink):
    b, s = q.shape[:2]
    nb = s // WINDOW
    qb = q.reshape(b, nb, WINDOW, N_KV_HEADS, GQA_GROUP, HEAD_DIM)
    pad = jnp.zeros((b, WINDOW, N_KV_HEADS, HEAD_DIM), k.dtype)
    kp = jnp.concatenate([pad, k], axis=1).reshape(b, nb + 1, WINDOW, N_KV_HEADS, HEAD_DIM)
    vp = jnp.concatenate([pad, v], axis=1).reshape(b, nb + 1, WINDOW, N_KV_HEADS, HEAD_DIM)
    kb = jnp.concatenate([kp[:, :-1], kp[:, 1:]], axis=2)
    vb = jnp.concatenate([vp[:, :-1], vp[:, 1:]], axis=2)
    scores = jnp.einsum('bnqkgd,bnskd->bnkgqs', qb, kb, preferred_element_type=jnp.float32) * (HEAD_DIM ** -0.5)
    blk = jnp.arange(nb)[:, None, None]
    qpos = blk * WINDOW + jnp.arange(WINDOW)[None, :, None]
    kpos = (blk - 1) * WINDOW + jnp.arange(2 * WINDOW)[None, None, :]
    mask = (kpos <= qpos) & (qpos - kpos < WINDOW) & (kpos >= 0)
    p = sink_softmax(scores, mask[None, :, None, None], sink)
    o = jnp.einsum('bnkgqs,bnskd->bnqkgd', p.astype(v.dtype), vb)
    return o.reshape(b, s, ATTN_WIDTH)


def window_attn_sample(q, k_new, v_new, cache_k, cache_v, sink):
    b, t = q.shape[:2]
    cw = cache_k.shape[1]
    kc = jnp.concatenate([cache_k, k_new], axis=1)
    vc = jnp.concatenate([cache_v, v_new], axis=1)
    qpos = PAST_LEN + jnp.arange(t)
    kpos = PAST_LEN - cw + jnp.arange(cw + t)
    mask = (kpos[None, :] <= qpos[:, None]) & (qpos[:, None] - kpos[None, :] < WINDOW)
    qg = q.reshape(b, t, N_KV_HEADS, GQA_GROUP, HEAD_DIM)
    scores = jnp.einsum('bqkgd,bskd->bkgqs', qg, kc, preferred_element_type=jnp.float32) * (HEAD_DIM ** -0.5)
    p = sink_softmax(scores, mask, sink)
    o = jnp.einsum('bkgqs,bskd->bqkgd', p.astype(vc.dtype), vc).reshape(b, t, ATTN_WIDTH)
    return o, kc[:, -cw:], vc[:, -cw:]


def causal_conv(xbc, conv_state, w, bias):
    xp = jnp.concatenate([conv_state, xbc], axis=1)
    y = lax.conv_general_dilated(xp, w[:, None, :], window_strides=(1,), padding='VALID',
                                 dimension_numbers=('NWC', 'WIO', 'NWC'), feature_group_count=CONV_DIM)
    return jax.nn.silu(y + bias), xp[:, -(CONV_W - 1):]


def ssd_chunked(x, dt, a, bm, cm):
    b, l = x.shape[:2]
    nc = l // CHUNK
    X = (x.astype(jnp.float32) * dt[..., None]).reshape(b, nc, CHUNK, N_SSM_GROUPS, HEADS_PER_GROUP, SSM_HEAD_DIM)
    dA = (dt * a).reshape(b, nc, CHUNK, N_SSM_GROUPS, HEADS_PER_GROUP)
    Bc = bm.astype(jnp.float32).reshape(b, nc, CHUNK, N_SSM_GROUPS, D_STATE)
    Cc = cm.astype(jnp.float32).reshape(b, nc, CHUNK, N_SSM_GROUPS, D_STATE)
    acum = jnp.cumsum(dA, axis=2)
    causal = jnp.tril(jnp.ones((CHUNK, CHUNK), dtype=bool))[None, None, :, :, None, None]
    seg = acum[:, :, :, None] - acum[:, :, None, :]
    lmat = jnp.exp(jnp.where(causal, seg, -jnp.inf))
    cb = jnp.einsum('bclgn,bcsgn->bclsg', Cc, Bc)
    y_diag = jnp.einsum('bclsgr,bcsgrp->bclgrp', cb[..., None] * lmat, X)
    decay_states = jnp.exp(acum[:, :, -1:] - acum)
    states = jnp.einsum('bclgn,bclgrp->bcgrpn', Bc, X * decay_states[..., None])
    chunk_decay = jnp.exp(acum[:, :, -1])

    def step(h, inp):
        dec, st = inp
        return h * dec[..., None, None] + st, h

    h0 = jnp.zeros((b, N_SSM_GROUPS, HEADS_PER_GROUP, SSM_HEAD_DIM, D_STATE), jnp.float32)
    h_last, h_prev = lax.scan(step, h0, (jnp.moveaxis(chunk_decay, 1, 0), jnp.moveaxis(states, 1, 0)))
    h_prev = jnp.moveaxis(h_prev, 0, 1)
    y_off = jnp.einsum('bclgn,bcgrpn->bclgrp', Cc, h_prev) * jnp.exp(acum)[..., None]
    y = (y_diag + y_off).reshape(b, l, N_SSM_HEADS, SSM_HEAD_DIM)
    return y, h_last.reshape(b, N_SSM_HEADS, SSM_HEAD_DIM, D_STATE)


def ssd_recurrent(x, dt, a, bm, cm, h0):
    b, l = x.shape[:2]
    X = (x.astype(jnp.float32) * dt[..., None]).reshape(b, l, N_SSM_GROUPS, HEADS_PER_GROUP, SSM_HEAD_DIM)
    da = jnp.exp(dt * a).reshape(b, l, N_SSM_GROUPS, HEADS_PER_GROUP)
    h = h0.astype(jnp.float32).reshape(b, N_SSM_GROUPS, HEADS_PER_GROUP, SSM_HEAD_DIM, D_STATE)

    def step(h, inp):
        xt, dat, bt, ct = inp
        h = h * dat[..., None, None] + jnp.einsum('bgrp,bgn->bgrpn', xt, bt)
        return h, jnp.einsum('bgrpn,bgn->bgrp', h, ct)

    seq = (jnp.moveaxis(X, 1, 0), jnp.moveaxis(da, 1, 0),
           jnp.moveaxis(bm.astype(jnp.float32), 1, 0), jnp.moveaxis(cm.astype(jnp.float32), 1, 0))
    h_last, ys = lax.scan(step, h, seq)
    y = jnp.moveaxis(ys, 0, 1).reshape(b, l, N_SSM_HEADS, SSM_HEAD_DIM)
    return y, h_last.reshape(b, N_SSM_HEADS, SSM_HEAD_DIM, D_STATE)


def token_mixers(hn, pos, lw, state):
    b, l = hn.shape[:2]
    proj = jnp.einsum('bld,de->ble', hn, lw['w_in'])
    q, k, v, z, xbc, dt_raw, gates = jnp.split(proj, SPLITS, axis=-1)
    q = rope(q.reshape(b, l, N_HEADS, HEAD_DIM), pos)
    k = rope(k.reshape(b, l, N_KV_HEADS, HEAD_DIM), pos)
    v = v.reshape(b, l, N_KV_HEADS, HEAD_DIM)
    if state is None:
        attn = window_attn_prompt(q, k, v, lw['attn_sinks'])
        win = min(WINDOW, l)
        new_k, new_v = k[:, -win:], v[:, -win:]
        conv_in = jnp.zeros((b, CONV_W - 1, CONV_DIM), xbc.dtype)
    else:
        attn, new_k, new_v = window_attn_sample(q, k, v, state['k'], state['v'], lw['attn_sinks'])
        conv_in = state['conv']
    xbc_act, new_conv = causal_conv(xbc, conv_in, lw['conv_w'], lw['conv_b'])
    xs, bm, cm = jnp.split(xbc_act, (D_INNER, D_INNER + N_SSM_GROUPS * D_STATE), axis=-1)
    xs = xs.reshape(b, l, N_SSM_HEADS, SSM_HEAD_DIM)
    bm = bm.reshape(b, l, N_SSM_GROUPS, D_STATE)
    cm = cm.reshape(b, l, N_SSM_GROUPS, D_STATE)
    dt = jax.nn.softplus(dt_raw.astype(jnp.float32) + lw['dt_bias'].astype(jnp.float32))
    a = -jnp.exp(lw['a_log'].astype(jnp.float32))
    if state is None:
        y, h_last = ssd_chunked(xs, dt, a, bm, cm)
    else:
        y, h_last = ssd_recurrent(xs, dt, a, bm, cm, state['ssm'])
    y = y + lw['d_skip'].astype(jnp.float32)[:, None] * xs.astype(jnp.float32)
    y = gated_group_rms_norm(y.reshape(b, l, D_INNER).astype(hn.dtype), z, lw['ssm_norm'])
    attn_d = jnp.einsum('ble,ed->bld', attn, lw['w_attn_branch'])
    ssm_d = jnp.einsum('ble,ed->bld', y, lw['w_ssm_branch'])
    g_attn, g_ssm = jnp.split(gates, N_BRANCH, axis=-1)
    merged = jax.nn.sigmoid(g_attn) * attn_d + jax.nn.sigmoid(g_ssm) * ssm_d
    out = jnp.einsum('bld,de->ble', merged, lw['w_out'])
    return out, (new_k, new_v, new_conv, h_last.astype(hn.dtype))


def swiglu_ffn(h, w_gate_up, w_down):
    g, u = jnp.split(jnp.einsum('bld,df->blf', h, w_gate_up), 2, axis=-1)
    return jnp.einsum('blf,fd->bld', jax.nn.silu(g) * u, w_down)


def decoder_layer(x, pos, lw, state):
    mix, new_state = token_mixers(rms_norm(x, lw['norm_mix_pre']), pos, lw, state)
    h = x + rms_norm(mix, lw['norm_mix_post'])
    f = swiglu_ffn(rms_norm(h, lw['norm_ffn_pre']), lw['w_gate_up'], lw['w_down'])
    return h + rms_norm(f, lw['norm_ffn_post']), new_state


def setup_inputs(seed: int = 0) -> dict:
    key = jax.random.key(seed)
    ks = jax.random.split(key, 24)
    f32 = jnp.float32
    cache_w = min(WINDOW, PAST_LEN)
    nrm = lambda k, shape, s: jax.random.normal(k, shape, f32) * s
    dt0 = jnp.exp(jax.random.uniform(ks[10], (DEPTH, N_SSM_HEADS), f32, np.log(1e-3), np.log(1e-1)))
    return {
        'x_prompt': nrm(ks[0], (BATCH, SEQ, D_MODEL), 1.0),
        'x_sample': nrm(ks[1], (DEC_BATCH, DEC_SEQ, D_MODEL), 1.0),
        'cache_win_k': nrm(ks[2], (DEPTH, DEC_BATCH, cache_w, N_KV_HEADS, HEAD_DIM), 1.0),
        'cache_win_v': nrm(ks[3], (DEPTH, DEC_BATCH, cache_w, N_KV_HEADS, HEAD_DIM), 1.0),
        'state_conv': nrm(ks[4], (DEPTH, DEC_BATCH, CONV_W - 1, CONV_DIM), 1.0),
        'state_ssm': nrm(ks[5], (DEPTH, DEC_BATCH, N_SSM_HEADS, SSM_HEAD_DIM, D_STATE), 0.5),
        'norm_mix_pre': 1.0 + nrm(ks[6], (DEPTH, D_MODEL), 0.02),
        'norm_mix_post': 1.0 + nrm(ks[7], (DEPTH, D_MODEL), 0.02),
        'w_in': nrm(ks[8], (DEPTH, D_MODEL, IN_PROJ_DIM), D_MODEL ** -0.5),
        'attn_sinks': nrm(ks[9], (DEPTH, N_HEADS), 1.0),
        'w_attn_branch': nrm(ks[11], (DEPTH, ATTN_WIDTH, D_MODEL), ATTN_WIDTH ** -0.5),
        'conv_w': nrm(ks[12], (DEPTH, CONV_W, CONV_DIM), CONV_W ** -0.5),
        'conv_b': nrm(ks[13], (DEPTH, CONV_DIM), 0.01),
        'dt_bias': dt0 + jnp.log(-jnp.expm1(-dt0)),
        'a_log': jnp.log(jax.random.uniform(ks[14], (DEPTH, N_SSM_HEADS), f32, 1.0, 16.0)),
        'd_skip': 1.0 + nrm(ks[15], (DEPTH, N_SSM_HEADS), 0.1),
        'ssm_norm': 1.0 + nrm(ks[16], (DEPTH, D_INNER), 0.02),
        'w_ssm_branch': nrm(ks[17], (DEPTH, D_INNER, D_MODEL), D_INNER ** -0.5),
        'w_out': nrm(ks[18], (DEPTH, D_MODEL, D_MODEL), D_MODEL ** -0.5),
        'norm_ffn_pre': 1.0 + nrm(ks[19], (DEPTH, D_MODEL), 0.02),
        'norm_ffn_post': 1.0 + nrm(ks[20], (DEPTH, D_MODEL), 0.02),
        'w_gate_up': nrm(ks[21], (DEPTH, D_MODEL, 2 * D_FF), D_MODEL ** -0.5),
        'w_down': nrm(ks[22], (DEPTH, D_FF, D_MODEL), D_FF ** -0.5),
    }


def reference(x_prompt, x_sample, cache_win_k, cache_win_v, state_conv, state_ssm,
              norm_mix_pre, norm_mix_post, w_in, attn_sinks, w_attn_branch, conv_w, conv_b,
              dt_bias, a_log, d_skip, ssm_norm, w_ssm_branch, w_out,
              norm_ffn_pre, norm_ffn_post, w_gate_up, w_down):
    pos_p = jnp.arange(x_prompt.shape[1], dtype=jnp.int32)
    pos_s = PAST_LEN + jnp.arange(x_sample.shape[1], dtype=jnp.int32)
    yp, ys = x_prompt, x_sample
    pk, pv, pc, ph, sk, sv, sc, sh = [], [], [], [], [], [], [], []
    for i in range(DEPTH):
        lw = {'norm_mix_pre': norm_mix_pre[i], 'norm_mix_post': norm_mix_post[i], 'w_in': w_in[i],
              'attn_sinks': attn_sinks[i], 'w_attn_branch': w_attn_branch[i], 'conv_w': conv_w[i],
              'conv_b': conv_b[i], 'dt_bias': dt_bias[i], 'a_log': a_log[i], 'd_skip': d_skip[i],
              'ssm_norm': ssm_norm[i], 'w_ssm_branch': w_ssm_branch[i], 'w_out': w_out[i],
              'norm_ffn_pre': norm_ffn_pre[i], 'norm_ffn_post': norm_ffn_post[i],
              'w_gate_up': w_gate_up[i], 'w_down': w_down[i]}
        yp, (k1, v1, c1, h1) = decoder_layer(yp, pos_p, lw, None)
        st = {'k': cache_win_k[i], 'v': cache_win_v[i], 'conv': state_conv[i], 'ssm': state_ssm[i]}
        ys, (k2, v2, c2, h2) = decoder_layer(ys, pos_s, lw, st)
        pk.append(k1); pv.append(v1); pc.append(c1); ph.append(h1)
        sk.append(k2); sv.append(v2); sc.append(c2); sh.append(h2)
    return (yp, ys,
            jnp.stack(pk), jnp.stack(pv), jnp.stack(pc), jnp.stack(ph),
            jnp.stack(sk), jnp.stack(sv), jnp.stack(sc), jnp.stack(sh))
```

```python
import functools

import numpy as np
import jax
import jax.numpy as jnp
from jax import lax
from jax.experimental import pallas as pl
from jax.experimental.pallas import tpu as pltpu

f32 = jnp.float32
bf16 = jnp.bfloat16

D_MODEL = 2048
BATCH = 4
SEQ = 2048
DEC_BATCH = 128
PAST_LEN = 16384
N_HEADS = 32
N_KV_HEADS = 8
HEAD_DIM = 64
GQA_GROUP = 4
WINDOW = 128
ROPE_THETA = 10000.0
ATTN_WIDTH = N_HEADS * HEAD_DIM
KV_WIDTH = N_KV_HEADS * HEAD_DIM
D_INNER = 4096
SSM_HEAD_DIM = 64
N_SSM_HEADS = 64
D_STATE = 128
N_SSM_GROUPS = 8
HEADS_PER_GROUP = 8
GROUP_WIDTH = HEADS_PER_GROUP * SSM_HEAD_DIM
CONV_W = 4
CONV_DIM = D_INNER + 2 * N_SSM_GROUPS * D_STATE
CHUNK = 128
D_FF = 5632
EPS = 1e-6

OFF_Q = 0
OFF_K = ATTN_WIDTH
OFF_V = OFF_K + KV_WIDTH
OFF_Z = OFF_V + KV_WIDTH
OFF_XBC = OFF_Z + D_INNER
OFF_DT = OFF_XBC + CONV_DIM
OFF_GATES = OFF_DT + N_SSM_HEADS
N_MAIN = OFF_DT

NP = BATCH * SEQ
NS = DEC_BATCH
TP = 1024
NTP = NP // TP
TR = 512
NTR = NP // TR
N_CHUNKS = SEQ // CHUNK

VMEM_LIMIT = 56 * 1024 * 1024


def _cparams(sem):
    return pltpu.CompilerParams(dimension_semantics=sem, vmem_limit_bytes=VMEM_LIMIT)


def _split2(v):
    hi = v.astype(bf16)
    lo = (v - hi.astype(f32)).astype(bf16)
    return hi, lo


def _split3(v):
    hi = v.astype(bf16)
    r = v - hi.astype(f32)
    mid = r.astype(bf16)
    lo = (r - mid.astype(f32)).astype(bf16)
    return hi, mid, lo


def _dot(a, b):
    return jnp.dot(a, b, preferred_element_type=f32)


def _dot_nt(a, b):
    return lax.dot_general(a, b, (((1,), (1,)), ((), ())), preferred_element_type=f32)


def _dot_tn(a, b):
    return lax.dot_general(a, b, (((0,), (0,)), ((), ())), preferred_element_type=f32)


def _dot_f32_lhs(a, b_exact):
    p = _split3(a)
    return _dot(p[0], b_exact) + _dot(p[1], b_exact) + _dot(p[2], b_exact)


def _dot_f32_rhs(a_exact, b):
    p = _split3(b)
    return _dot(a_exact, p[0]) + _dot(a_exact, p[1]) + _dot(a_exact, p[2])


def _silu(x):
    return x * (1.0 / (1.0 + jnp.exp(-x)))


def _sigmoid(x):
    return 1.0 / (1.0 + jnp.exp(-x))


def _softplus(x):
    return jnp.maximum(x, 0.0) + jnp.log1p(jnp.exp(-jnp.abs(x)))


def _rms(x, w):
    return x * lax.rsqrt(jnp.mean(x * x, axis=-1, keepdims=True) + EPS) * w


def _head_selector(n_heads, width):
    r = lax.broadcasted_iota(jnp.int32, (n_heads, n_heads * width), 0)
    c = lax.broadcasted_iota(jnp.int32, (n_heads, n_heads * width), 1)
    return jnp.where(c // width == r, 1.0, 0.0).astype(bf16)


def _prenorm_body(xp_ref, xs_ref, nw_ref, wdt_ref, wdtt_ref, dtb_ref, dtbt_ref,
                  hnp_ref, hns_ref, dtg_ref, dtt_ref, dts_ref):
    i = pl.program_id(0)
    w_hi, w_lo = _split2(wdt_ref[...])

    def dt_rows(hn):
        a_hi, a_lo = _split2(hn)
        raw = _dot(a_hi, w_hi) + _dot(a_hi, w_lo) + _dot(a_lo, w_hi)
        return _softplus(raw + dtb_ref[...])

    @pl.when(i < NTP)
    def _():
        hn = _rms(xp_ref[...], nw_ref[...])
        hnp_ref[...] = hn.astype(bf16)
        dt = dt_rows(hn)
        for g in range(N_SSM_GROUPS):
            dtg_ref[g] = dt[:, g * HEADS_PER_GROUP:(g + 1) * HEADS_PER_GROUP]
        a_hi, a_lo = _split2(hn)
        wt_hi, wt_lo = _split2(wdtt_ref[...])
        raw_t = _dot_nt(wt_hi, a_hi) + _dot_nt(wt_lo, a_hi) + _dot_nt(wt_hi, a_lo)
        dtt_ref[...] = _softplus(raw_t + dtbt_ref[...])

    @pl.when(i == NTP)
    def _():
        hn = _rms(xs_ref[...], nw_ref[...])
        hns_ref[...] = hn
        dts_ref[...] = dt_rows(hn)


def _prenorm(xp, xs, norm_w, w_dt, dt_bias):
    pidx = lambda i: (jnp.minimum(i, NTP - 1), 0)
    const = lambda i: (0, 0)
    return pl.pallas_call(
        _prenorm_body,
        grid=(NTP + 1,),
        in_specs=[
            pl.BlockSpec((TP, D_MODEL), pidx),
            pl.BlockSpec((NS, D_MODEL), const),
            pl.BlockSpec((1, D_MODEL), const),
            pl.BlockSpec((D_MODEL, N_SSM_HEADS), const),
            pl.BlockSpec((N_SSM_HEADS, D_MODEL), const),
            pl.BlockSpec((1, N_SSM_HEADS), const),
            pl.BlockSpec((N_SSM_HEADS, 1), const),
        ],
        out_specs=[
            pl.BlockSpec((TP, D_MODEL), pidx),
            pl.BlockSpec((NS, D_MODEL), const),
            pl.BlockSpec((N_SSM_GROUPS, TP, HEADS_PER_GROUP), lambda i: (0, jnp.minimum(i, NTP - 1), 0)),
            pl.BlockSpec((N_SSM_HEADS, TP), lambda i: (0, jnp.minimum(i, NTP - 1))),
            pl.BlockSpec((NS, N_SSM_HEADS), const),
        ],
        out_shape=[
            jax.ShapeDtypeStruct((NP, D_MODEL), bf16),
            jax.ShapeDtypeStruct((NS, D_MODEL), f32),
            jax.ShapeDtypeStruct((N_SSM_GROUPS, NP, HEADS_PER_GROUP), f32),
            jax.ShapeDtypeStruct((N_SSM_HEADS, NP), f32),
            jax.ShapeDtypeStruct((NS, N_SSM_HEADS), f32),
        ],
        compiler_params=_cparams(("arbitrary",)),
        name="prenorm_dt",
    )(xp, xs, norm_w.reshape(1, D_MODEL), w_dt, w_dt.T, dt_bias.reshape(1, -1), dt_bias.reshape(-1, 1))


def _rows_matmul_body(*refs, n_a, n_w, n_row, n_col, n_out, dots, epilogue, cast_rows):
    pos = 0
    a_p = refs[pos:pos + n_a]; pos += n_a
    a_s = refs[pos:pos + n_a]; pos += n_a
    w = refs[pos:pos + n_w]; pos += n_w
    row_p = refs[pos:pos + n_row]; pos += n_row
    row_s = refs[pos:pos + n_row]; pos += n_row
    col = refs[pos:pos + n_col]; pos += n_col
    out_p = refs[pos:pos + n_out]; pos += n_out
    out_s = refs[pos:pos + n_out]; pos += n_out
    wb = refs[pos:pos + n_w]
    i = pl.program_id(1)

    @pl.when(i == 0)
    def _():
        for k in range(n_w):
            rows = w[k].shape[0]

            def cast(c, carry, k=k):
                r0 = pl.multiple_of(c * cast_rows, cast_rows)
                wb[k][pl.ds(r0, cast_rows), :] = w[k][pl.ds(r0, cast_rows), :].astype(bf16)
                return carry

            lax.fori_loop(0, rows // cast_rows, cast, 0)

    def compute(a, row, out):
        rs = [_dot(a[ai][...].astype(bf16), wb[wi][...]) for ai, wi in dots]
        res = epilogue(rs, [r[...].astype(f32) for r in row], [c[...] for c in col])
        for o, v in zip(out, res):
            o[...] = v.astype(o.dtype)

    @pl.when(i < NTP)
    def _():
        compute(a_p, row_p, out_p)

    @pl.when(i == NTP)
    def _():
        compute(a_s, row_s, out_s)


def _rows_matmul(name, a_pairs, weights, dots, epilogue, out_dtypes, tn, n_panels,
                 row_pairs=(), cols=()):
    n_a, n_w, n_row, n_col, n_out = len(a_pairs), len(weights), len(row_pairs), len(cols), len(out_dtypes)
    pidx = lambda n, i: (jnp.minimum(i, NTP - 1), 0)
    sidx = lambda n, i: (0, 0)
    in_specs, args = [], []
    for ap, _ in a_pairs:
        in_specs.append(pl.BlockSpec((TP, ap.shape[1]), pidx)); args.append(ap)
    for _, as_ in a_pairs:
        in_specs.append(pl.BlockSpec((NS, as_.shape[1]), sidx)); args.append(as_)
    for wk, off in weights:
        in_specs.append(pl.BlockSpec((wk.shape[0], tn), functools.partial(lambda n, i, off: (0, n + off), off=off)))
        args.append(wk)
    out_p_idx = lambda n, i: (jnp.minimum(i, NTP - 1), n)
    out_s_idx = lambda n, i: (0, n)
    for rp, _, off in row_pairs:
        in_specs.append(pl.BlockSpec((TP, tn), functools.partial(
            lambda n, i, off: (jnp.minimum(i, NTP - 1), n + off), off=off)))
        args.append(rp)
    for _, rs, off in row_pairs:
        in_specs.append(pl.BlockSpec((NS, tn), functools.partial(lambda n, i, off: (0, n + off), off=off)))
        args.append(rs)
    for c in cols:
        in_specs.append(pl.BlockSpec((1, tn), lambda n, i: (0, n))); args.append(c)
    width = n_panels * tn
    out_specs = ([pl.BlockSpec((TP, tn), out_p_idx)] * n_out + [pl.BlockSpec((NS, tn), out_s_idx)] * n_out)
    out_shape = ([jax.ShapeDtypeStruct((NP, width), dp) for dp, _ in out_dtypes]
                 + [jax.ShapeDtypeStruct((NS, width), ds) for _, ds in out_dtypes])
    scratch = [pltpu.VMEM((wk.shape[0], tn), bf16) for wk, _ in weights]
    body = functools.partial(_rows_matmul_body, n_a=n_a, n_w=n_w, n_row=n_row, n_col=n_col, n_out=n_out,
                             dots=tuple(dots), epilogue=epilogue, cast_rows=256)
    res = pl.pallas_call(
        body,
        grid=(n_panels, NTP + 1),
        in_specs=in_specs,
        out_specs=out_specs,
        out_shape=out_shape,
        scratch_shapes=scratch,
        compiler_params=_cparams(("arbitrary", "arbitrary")),
        name=name,
    )(*args)
    return [(res[k], res[n_out + k]) for k in range(n_out)]


def _rope_tables(pos):
    half = HEAD_DIM // 2
    inv = ROPE_THETA ** (-jnp.arange(half, dtype=f32) / half)
    ang = pos.astype(f32)[:, None] * inv[None, :]
    cos, sin = jnp.cos(ang), jnp.sin(ang)
    return jnp.concatenate([cos, cos, cos, cos], axis=1), jnp.concatenate([-sin, sin, -sin, sin], axis=1)


def _rope_chunk(x, cos, sin, first_half):
    from_right = pltpu.roll(x, 96, 1)
    from_left = pltpu.roll(x, 32, 1)
    return x * cos + jnp.where(first_half, from_right, from_left) * sin


def _rope(x, cos, sin):
    rows, width = x.shape
    lane = lax.broadcasted_iota(jnp.int32, (rows, 128), 1)
    first_half = (lane % HEAD_DIM) < (HEAD_DIM // 2)
    return [_rope_chunk(x[:, c * 128:(c + 1) * 128], cos, sin, first_half) for c in range(width // 128)]


def _attn_prompt_body(sink_ref, q_ref, kv_ref, cos_ref, sin_ref, bias_ref, o_ref, wk_ref, kcat, vcat):
    n = pl.program_id(1)

    @pl.when(n == 0)
    def _():
        kcat[0:WINDOW, :] = jnp.zeros((WINDOW, KV_WIDTH), bf16)
        vcat[0:WINDOW, :] = jnp.zeros((WINDOW, KV_WIDTH), bf16)

    cos, sin = cos_ref[...], sin_ref[...]
    qc = _rope(q_ref[...].astype(f32), cos, sin)
    kc = _rope(kv_ref[:, 0:KV_WIDTH].astype(f32), cos, sin)
    for c, v in enumerate(kc):
        kcat[WINDOW:2 * WINDOW, c * 128:(c + 1) * 128] = v.astype(bf16)
    vcat[WINDOW:2 * WINDOW, :] = kv_ref[:, KV_WIDTH:2 * KV_WIDTH]

    @pl.when(n == N_CHUNKS - 1)
    def _():
        for c, v in enumerate(kc):
            wk_ref[:, c * 128:(c + 1) * 128] = v

    bias = bias_ref[...]
    scale = HEAD_DIM ** -0.5
    for g in range(N_KV_HEADS):
        kg = kcat[:, g * HEAD_DIM:(g + 1) * HEAD_DIM]
        vg = vcat[:, g * HEAD_DIM:(g + 1) * HEAD_DIM]
        for j in range(GQA_GROUP):
            h = g * GQA_GROUP + j
            qh = (qc[h // 2][:, (h % 2) * HEAD_DIM:(h % 2 + 1) * HEAD_DIM] * scale).astype(bf16)
            s = _dot_nt(qh, kg) + bias
            sink = sink_ref[h]
            m = jnp.maximum(jnp.max(s, axis=1, keepdims=True), sink)
            p = jnp.exp(s - m)
            den = jnp.sum(p, axis=1, keepdims=True) + jnp.exp(sink - m)
            o = _dot(p.astype(bf16), vg) * (1.0 / den)
            o_ref[:, h * HEAD_DIM:(h + 1) * HEAD_DIM] = o.astype(o_ref.dtype)

    kcat[0:WINDOW, :] = kcat[WINDOW:2 * WINDOW, :]
    vcat[0:WINDOW, :] = vcat[WINDOW:2 * WINDOW, :]


def _attn_prompt(proj_p, sinks, cos, sin, bias):
    row = lambda b, n: b * N_CHUNKS + n
    return pl.pallas_call(
        _attn_prompt_body,
        grid=(BATCH, N_CHUNKS),
        in_specs=[
            pl.BlockSpec(memory_space=pltpu.SMEM),
            pl.BlockSpec((CHUNK, ATTN_WIDTH), lambda b, n: (row(b, n), 0)),
            pl.BlockSpec((CHUNK, 2 * KV_WIDTH), lambda b, n: (row(b, n), OFF_K // (2 * KV_WIDTH))),
            pl.BlockSpec((CHUNK, 128), lambda b, n: (n, 0)),
            pl.BlockSpec((CHUNK, 128), lambda b, n: (n, 0)),
            pl.BlockSpec((None, CHUNK, 2 * WINDOW), lambda b, n: (jnp.minimum(n, 1), 0, 0)),
        ],
        out_specs=[
            pl.BlockSpec((CHUNK, ATTN_WIDTH), lambda b, n: (row(b, n), 0)),
            pl.BlockSpec((None, WINDOW, KV_WIDTH), lambda b, n: (b, 0, 0)),
        ],
        out_shape=[
            jax.ShapeDtypeStruct((NP, ATTN_WIDTH), bf16),
            jax.ShapeDtypeStruct((BATCH, WINDOW, KV_WIDTH), f32),
        ],
        scratch_shapes=[pltpu.VMEM((2 * WINDOW, KV_WIDTH), bf16), pltpu.VMEM((2 * WINDOW, KV_WIDTH), bf16)],
        compiler_params=_cparams(("arbitrary", "arbitrary")),
        name="attn_prompt",
    )(sinks, proj_p, proj_p, cos, sin, bias)


def _band_bias():
    qi = np.arange(WINDOW)[:, None]
    kj = np.arange(2 * WINDOW)[None, :]
    cur = (kj >= WINDOW) & (kj - WINDOW <= qi)
    prev = (kj < WINDOW) & (kj > qi)
    neg = np.float32(-np.inf)
    b0 = np.where(cur, np.float32(0), neg)
    b1 = np.where(cur | prev, np.float32(0), neg)
    return jnp.asarray(np.stack([b0, b1]).astype(np.float32))


SB = 8


def _dup_halves(x, lane_first):
    swapped = pltpu.roll(x, 64, x.ndim - 1)
    return jnp.where(lane_first, x, swapped), jnp.where(lane_first, swapped, x)


def _attn_sample_body(q_ref, k_ref, v_ref, ck_ref, cv_ref, cos_ref, sin_ref, sink_ref, e_ref, et_ref,
                      o_ref, wk_ref, wv_ref):
    cos, sin = cos_ref[...], sin_ref[...]
    scale = HEAD_DIM ** -0.5
    qc = [c * scale for c in _rope(q_ref[...], cos, sin)]
    kc = _rope(k_ref[...], cos, sin)
    vn = v_ref[...]
    cw = ck_ref.shape[1]
    lane3 = lax.broadcasted_iota(jnp.int32, (SB, cw, 128), 2) < HEAD_DIM
    lane2 = lax.broadcasted_iota(jnp.int32, (SB, 128), 1) < HEAD_DIM

    s = jnp.zeros((SB * cw, 128), f32)
    sn = jnp.zeros((SB, 128), f32)
    for v in range(KV_WIDTH // 128):
        kd = _dup_halves(ck_ref[:, :, v * 128:(v + 1) * 128], lane3)
        knd = _dup_halves(kc[v], lane2)
        for half in range(2):
            for t in range(2):
                c = 4 * v + 2 * half + t
                e_c = e_ref[c * 128:(c + 1) * 128, :]
                prod = (kd[half] * qc[c][:, None, :]).astype(bf16).reshape(SB * cw, 128)
                s = s + _dot(prod, e_c)
                sn = sn + _dot((knd[half] * qc[c]).astype(bf16), e_c)
    s = s.reshape(SB, cw, 128)
    key = lax.broadcasted_iota(jnp.int32, (SB, cw, 128), 1)
    s = jnp.where(key == 0, -jnp.inf, s)
    sink = sink_ref[...]
    m = jnp.maximum(jnp.maximum(jnp.max(s, axis=1), sn), sink)
    p = jnp.exp(s - m[:, None, :])
    pn = jnp.exp(sn - m)
    den = jnp.sum(p, axis=1) + pn + jnp.exp(sink - m)
    inv = 1.0 / den
    p = (p * inv[:, None, :]).astype(bf16).reshape(SB * cw, 128)
    pn = (pn * inv).astype(bf16)

    for v in range(KV_WIDTH // 128):
        vd = _dup_halves(cv_ref[:, :, v * 128:(v + 1) * 128], lane3)
        vnd = _dup_halves(vn[:, v * 128:(v + 1) * 128], lane2)
        for half in range(2):
            for t in range(2):
                c = 4 * v + 2 * half + t
                et_c = et_ref[:, c * 128:(c + 1) * 128]
                pe = _dot(p, et_c).reshape(SB, cw, 128)
                pne = _dot(pn, et_c)
                o_ref[:, c * 128:(c + 1) * 128] = jnp.sum(pe * vd[half], axis=1) + pne * vnd[half]

    wk_ref[:, 0:cw - 1, :] = ck_ref[:, 1:cw, :]
    wv_ref[:, 0:cw - 1, :] = cv_ref[:, 1:cw, :]
    for c, val in enumerate(kc):
        wk_ref[:, cw - 1:cw, c * 128:(c + 1) * 128] = val[:, None, :]
    wv_ref[:, cw - 1:cw, :] = vn[:, None, :]


def _attn_sample(q, k, v, cache_k, cache_v, cos, sin, sinks):
    cw = cache_k.shape[1]
    sel = np.zeros((ATTN_WIDTH, 128), np.float32)
    sel[np.arange(ATTN_WIDTH), np.arange(ATTN_WIDTH) // HEAD_DIM] = 1.0
    e = jnp.asarray(sel, dtype=bf16)
    et = jnp.asarray(sel.T, dtype=bf16)
    sink_row = jnp.zeros((1, 128), f32).at[0, :N_HEADS].set(sinks)
    rows = lambda i: (i, 0)
    rows3 = lambda i: (i, 0, 0)
    const = lambda i: (0, 0)
    return pl.pallas_call(
        _attn_sample_body,
        grid=(NS // SB,),
        in_specs=[
            pl.BlockSpec((SB, ATTN_WIDTH), rows),
            pl.BlockSpec((SB, KV_WIDTH), rows),
            pl.BlockSpec((SB, KV_WIDTH), rows),
            pl.BlockSpec((SB, cw, KV_WIDTH), rows3),
            pl.BlockSpec((SB, cw, KV_WIDTH), rows3),
            pl.BlockSpec((1, 128), const),
            pl.BlockSpec((1, 128), const),
            pl.BlockSpec((1, 128), const),
            pl.BlockSpec((ATTN_WIDTH, 128), const),
            pl.BlockSpec((128, ATTN_WIDTH), const),
        ],
        out_specs=[
            pl.BlockSpec((SB, ATTN_WIDTH), rows),
            pl.BlockSpec((SB, cw, KV_WIDTH), rows3),
            pl.BlockSpec((SB, cw, KV_WIDTH), rows3),
        ],
        out_shape=[
            jax.ShapeDtypeStruct((NS, ATTN_WIDTH), f32),
            jax.ShapeDtypeStruct((NS, cw, KV_WIDTH), f32),
            jax.ShapeDtypeStruct((NS, cw, KV_WIDTH), f32),
        ],
        compiler_params=_cparams(("arbitrary",)),
        name="attn_sample",
    )(q, k, v, cache_k, cache_v, cos, sin, sink_row, e, et)


HALO = 8


def _ssd_prompt_body(x_ref, b_ref, c_ref, z_ref, dt_ref, dtt_ref, cwx_ref, cwb_ref, cwc_ref,
                     cbx_ref, cbb_ref, cbc_ref, alog_ref, alogt_ref, dsk_ref, nw_ref,
                     y_ref, st_ref, hst, xh, bh, ch):
    c = pl.program_id(2)

    @pl.when(c == 0)
    def _():
        hst[...] = jnp.zeros_like(hst)
        xh[0:HALO, :] = jnp.zeros((HALO, GROUP_WIDTH), f32)
        bh[0:HALO, :] = jnp.zeros((HALO, D_STATE), f32)
        ch[0:HALO, :] = jnp.zeros((HALO, D_STATE), f32)

    def conv(buf, new_ref, w_ref, bias_ref):
        buf[HALO:HALO + CHUNK, :] = new_ref[...].astype(f32)
        w = w_ref[...]
        acc = bias_ref[...] + w[3:4, :] * buf[HALO:HALO + CHUNK, :]
        for j in range(1, CONV_W):
            acc = acc + w[3 - j:4 - j, :] * buf[HALO - j:HALO - j + CHUNK, :]
        buf[0:HALO, :] = buf[CHUNK:CHUNK + HALO, :]
        return _silu(acc)

    xs = conv(xh, x_ref, cwx_ref, cbx_ref)
    bm = conv(bh, b_ref, cwb_ref, cbb_ref)
    cm = conv(ch, c_ref, cwc_ref, cbc_ref)
    bmb, cmb = bm.astype(bf16), cm.astype(bf16)

    dt = dt_ref[...]
    da = dt * (-jnp.exp(alog_ref[...]))
    dat = dtt_ref[...] * (-jnp.exp(alogt_ref[...]))
    li = lax.broadcasted_iota(jnp.int32, (CHUNK, CHUNK), 0)
    si = lax.broadcasted_iota(jnp.int32, (CHUNK, CHUNK), 1)
    causal = li >= si
    tri = jnp.where(causal, 1.0, 0.0).astype(bf16)
    trit = jnp.where(li <= si, 1.0, 0.0).astype(bf16)
    acum = _dot_f32_rhs(tri, da)
    acumt = _dot_f32_lhs(dat, trit)
    a_last = acum[CHUNK - 1:CHUNK, :]

    sel = _head_selector(HEADS_PER_GROUP, SSM_HEAD_DIM)
    expand = lambda v: _dot_f32_lhs(v, sel)
    expand_row = lambda v: _dot_f32_lhs(jnp.broadcast_to(v, (8, HEADS_PER_GROUP)), sel)[0:1, :]

    xdt = (xs * expand(dt)).astype(bf16)
    xdec = (xs * expand(dt * jnp.exp(a_last - acum))).astype(bf16)

    cb = _dot_nt(cmb, bmb)
    lane = lax.broadcasted_iota(jnp.int32, (CHUNK, 128), 1)
    zero = jnp.zeros((CHUNK, 128), bf16)
    pairs = []
    for pr in range(HEADS_PER_GROUP // 2):
        ms = []
        for r in (2 * pr, 2 * pr + 1):
            seg = acum[:, r:r + 1] - acumt[r:r + 1, :]
            ms.append((cb * jnp.exp(jnp.where(causal, seg, -jnp.inf))).astype(bf16))
        xp = xdt[:, pr * 128:(pr + 1) * 128]
        x0 = jnp.where(lane < SSM_HEAD_DIM, xp, zero)
        x1 = jnp.where(lane < SSM_HEAD_DIM, zero, xp)
        pairs.append(_dot(jnp.concatenate(ms, axis=1), jnp.concatenate([x0, x1], axis=0)))
    y = jnp.concatenate(pairs, axis=1)

    h_prev = hst[...]
    y = y + _dot(cmb, h_prev.astype(bf16)) * expand(jnp.exp(acum))
    y = y + expand_row(dsk_ref[...]) * xs

    gz = y * _silu(z_ref[...].astype(f32))
    y_ref[...] = _rms(gz, nw_ref[...]).astype(y_ref.dtype)

    h_new = h_prev * expand_row(jnp.exp(a_last)) + _dot_tn(bmb, xdec)
    hst[...] = h_new

    @pl.when(c == N_CHUNKS - 1)
    def _():
        st_ref[...] = h_new.T


def _ssd_prompt(proj_p, dtg, dtt, conv_w, conv_b, a_log, d_skip, ssm_norm):
    row = lambda b, g, c: b * N_CHUNKS + c
    xoff = OFF_XBC // GROUP_WIDTH
    boff = (OFF_XBC + D_INNER) // D_STATE
    coff = boff + N_SSM_GROUPS
    zoff = OFF_Z // GROUP_WIDTH
    wboff = D_INNER // D_STATE
    alog_g = a_log.reshape(N_SSM_GROUPS, 1, HEADS_PER_GROUP)
    alog_t = a_log.reshape(N_SSM_GROUPS, HEADS_PER_GROUP, 1)
    dsk_g = d_skip.reshape(N_SSM_GROUPS, 1, HEADS_PER_GROUP)
    conv_b = conv_b.reshape(1, CONV_DIM)
    grp = lambda b, g, c: (g, 0, 0)
    return pl.pallas_call(
        _ssd_prompt_body,
        grid=(BATCH, N_SSM_GROUPS, N_CHUNKS),
        in_specs=[
            pl.BlockSpec((CHUNK, GROUP_WIDTH), lambda b, g, c: (row(b, g, c), xoff + g)),
            pl.BlockSpec((CHUNK, D_STATE), lambda b, g, c: (row(b, g, c), boff + g)),
            pl.BlockSpec((CHUNK, D_STATE), lambda b, g, c: (row(b, g, c), coff + g)),
            pl.BlockSpec((CHUNK, GROUP_WIDTH), lambda b, g, c: (row(b, g, c), zoff + g)),
            pl.BlockSpec((None, CHUNK, HEADS_PER_GROUP), lambda b, g, c: (g, row(b, g, c), 0)),
            pl.BlockSpec((HEADS_PER_GROUP, CHUNK), lambda b, g, c: (g, row(b, g, c))),
            pl.BlockSpec((CONV_W, GROUP_WIDTH), lambda b, g, c: (0, g)),
            pl.BlockSpec((CONV_W, D_STATE), lambda b, g, c: (0, wboff + g)),
            pl.BlockSpec((CONV_W, D_STATE), lambda b, g, c: (0, wboff + N_SSM_GROUPS + g)),
            pl.BlockSpec((1, GROUP_WIDTH), lambda b, g, c: (0, g)),
            pl.BlockSpec((1, D_STATE), lambda b, g, c: (0, wboff + g)),
            pl.BlockSpec((1, D_STATE), lambda b, g, c: (0, wboff + N_SSM_GROUPS + g)),
            pl.BlockSpec((None, 1, HEADS_PER_GROUP), grp),
            pl.BlockSpec((None, HEADS_PER_GROUP, 1), grp),
            pl.BlockSpec((None, 1, HEADS_PER_GROUP), grp),
            pl.BlockSpec((1, GROUP_WIDTH), lambda b, g, c: (0, g)),
        ],
        out_specs=[
            pl.BlockSpec((CHUNK, GROUP_WIDTH), lambda b, g, c: (row(b, g, c), g)),
            pl.BlockSpec((None, None, GROUP_WIDTH, D_STATE), lambda b, g, c: (b, g, 0, 0)),
        ],
        out_shape=[
            jax.ShapeDtypeStruct((NP, D_INNER), bf16),
            jax.ShapeDtypeStruct((BATCH, N_SSM_GROUPS, GROUP_WIDTH, D_STATE), f32),
        ],
        scratch_shapes=[
            pltpu.VMEM((D_STATE, GROUP_WIDTH), f32),
            pltpu.VMEM((HALO + CHUNK, GROUP_WIDTH), f32),
            pltpu.VMEM((HALO + CHUNK, D_STATE), f32),
            pltpu.VMEM((HALO + CHUNK, D_STATE), f32),
        ],
        compiler_params=_cparams(("arbitrary", "arbitrary", "arbitrary")),
        name="ssd_prompt",
    )(proj_p, proj_p, proj_p, proj_p, dtg, dtt, conv_w, conv_w, conv_w, conv_b, conv_b, conv_b,
      alog_g, alog_t, dsk_g, ssm_norm.reshape(1, D_INNER))


def _ssd_sample_body(xbc_ref, z_ref, dt_ref, sc_ref, h_ref, cw_ref, cb_ref, alog_ref, dsk_ref, nw_ref,
                     y_ref, sco_ref, ho_ref):
    sc = sc_ref[...]
    xn = xbc_ref[...]
    w = cw_ref[...]
    acc = cb_ref[...] + w[3:4, :] * xn
    for j in range(CONV_W - 1):
        acc = acc + w[j:j + 1, :] * sc[j:j + 1, :]
    act = _silu(acc)
    sco_ref[0:CONV_W - 2, :] = sc[1:CONV_W - 1, :]
    sco_ref[CONV_W - 2:CONV_W - 1, :] = xn

    xs = act[:, 0:D_INNER]
    dt = dt_ref[...]
    decay = jnp.exp(dt * (-jnp.exp(alog_ref[...])))
    sel = _head_selector(N_SSM_HEADS, SSM_HEAD_DIM)
    rows8 = lambda v: jnp.broadcast_to(v, (8, v.shape[1]))
    expand = lambda v: _dot_f32_lhs(rows8(v), sel)[0:1, :]
    xdt = xs * expand(dt)
    dec = expand(decay)
    dsk = expand(dsk_ref[...])

    x_hi, x_lo = _split2(xdt)
    d_hi, d_mid, d_lo = _split3(dec)
    ones = jnp.ones((1, D_STATE), bf16)
    zeros = jnp.zeros((1, D_STATE), bf16)
    ys = []
    for g in range(N_SSM_GROUPS):
        cols = slice(g * GROUP_WIDTH, (g + 1) * GROUP_WIDTH)
        bg = act[:, D_INNER + g * D_STATE:D_INNER + (g + 1) * D_STATE]
        cg = act[:, D_INNER + (N_SSM_GROUPS + g) * D_STATE:D_INNER + (N_SSM_GROUPS + g + 1) * D_STATE]
        b_hi, b_lo = _split2(bg)
        lhs = jnp.concatenate([x_hi[:, cols], x_hi[:, cols], x_lo[:, cols],
                               d_hi[:, cols], d_mid[:, cols], d_lo[:, cols],
                               jnp.zeros((2, GROUP_WIDTH), bf16)], axis=0)
        rhs = jnp.concatenate([jnp.concatenate([b_hi, zeros], axis=1), jnp.concatenate([b_lo, zeros], axis=1),
                               jnp.concatenate([b_hi, zeros], axis=1)]
                              + [jnp.concatenate([zeros, ones], axis=1)] * 3
                              + [jnp.zeros((2, 2 * D_STATE), bf16)], axis=0)
        both = _dot_tn(lhs, rhs)
        h_new = h_ref[cols, :] * both[:, D_STATE:] + both[:, 0:D_STATE]
        ho_ref[cols, :] = h_new
        ys.append(_dot_nt(rows8(cg).astype(bf16), h_new.astype(bf16))[0:1, :])
    zs = _silu(z_ref[...])
    for g in range(N_SSM_GROUPS):
        cols = slice(g * GROUP_WIDTH, (g + 1) * GROUP_WIDTH)
        gz = (ys[g] + dsk[:, cols] * xs[:, cols]) * zs[:, cols]
        y_ref[:, cols] = _rms(gz, nw_ref[:, cols])


def _ssd_sample(xbc, z, dt, state_conv, state_ssm, conv_w, conv_b, a_log, d_skip, ssm_norm):
    rows = lambda i: (i, 0, 0)
    const = lambda i: (0, 0)
    return pl.pallas_call(
        _ssd_sample_body,
        grid=(NS,),
        in_specs=[
            pl.BlockSpec((None, 1, CONV_DIM), rows),
            pl.BlockSpec((None, 1, D_INNER), rows),
            pl.BlockSpec((None, 1, N_SSM_HEADS), rows),
            pl.BlockSpec((None, CONV_W - 1, CONV_DIM), rows),
            pl.BlockSpec((None, D_INNER, D_STATE), rows),
            pl.BlockSpec((CONV_W, CONV_DIM), const),
            pl.BlockSpec((1, CONV_DIM), const),
            pl.BlockSpec((1, N_SSM_HEADS), const),
            pl.BlockSpec((1, N_SSM_HEADS), const),
            pl.BlockSpec((1, D_INNER), const),
        ],
        out_specs=[
            pl.BlockSpec((None, 1, D_INNER), rows),
            pl.BlockSpec((None, CONV_W - 1, CONV_DIM), rows),
            pl.BlockSpec((None, D_INNER, D_STATE), rows),
        ],
        out_shape=[
            jax.ShapeDtypeStruct((NS, 1, D_INNER), f32),
            jax.ShapeDtypeStruct((NS, CONV_W - 1, CONV_DIM), f32),
            jax.ShapeDtypeStruct((NS, D_INNER, D_STATE), f32),
        ],
        compiler_params=_cparams(("arbitrary",)),
        name="ssd_sample",
    )(xbc.reshape(NS, 1, CONV_DIM), z.reshape(NS, 1, D_INNER), dt.reshape(NS, 1, N_SSM_HEADS),
      state_conv, state_ssm.reshape(NS, D_INNER, D_STATE), conv_w, conv_b.reshape(1, CONV_DIM),
      a_log.reshape(1, -1), d_skip.reshape(1, -1), ssm_norm.reshape(1, -1))


def _resid_norm_body(*refs, n_out):
    mp, ms, xp, xs, w1, w2 = refs[:6]
    outs = refs[6:]
    i = pl.program_id(0)

    def compute(m, x, o):
        h = x[...] + _rms(m[...].astype(f32), w1[...])
        o[0][...] = h
        if n_out == 2:
            o[1][...] = _rms(h, w2[...]).astype(o[1].dtype)

    @pl.when(i < NTR)
    def _():
        compute(mp, xp, outs[0::2] if n_out == 2 else outs[0:1])

    @pl.when(i == NTR)
    def _():
        compute(ms, xs, outs[1::2] if n_out == 2 else outs[1:2])


def _resid_norm(name, m_pair, x_pair, w_post, w_next):
    n_out = 2 if w_next is not None else 1
    pidx = lambda i: (jnp.minimum(i, NTR - 1), 0)
    const = lambda i: (0, 0)
    w2 = w_next if w_next is not None else w_post
    out_specs = [pl.BlockSpec((TR, D_MODEL), pidx), pl.BlockSpec((NS, D_MODEL), const)]
    out_shape = [jax.ShapeDtypeStruct((NP, D_MODEL), f32), jax.ShapeDtypeStruct((NS, D_MODEL), f32)]
    if n_out == 2:
        out_specs += [pl.BlockSpec((TR, D_MODEL), pidx), pl.BlockSpec((NS, D_MODEL), const)]
        out_shape += [jax.ShapeDtypeStruct((NP, D_MODEL), bf16), jax.ShapeDtypeStruct((NS, D_MODEL), f32)]
    return pl.pallas_call(
        functools.partial(_resid_norm_body, n_out=n_out),
        grid=(NTR + 1,),
        in_specs=[
            pl.BlockSpec((TR, D_MODEL), pidx), pl.BlockSpec((NS, D_MODEL), const),
            pl.BlockSpec((TR, D_MODEL), pidx), pl.BlockSpec((NS, D_MODEL), const),
            pl.BlockSpec((1, D_MODEL), const), pl.BlockSpec((1, D_MODEL), const),
        ],
        out_specs=out_specs,
        out_shape=out_shape,
        compiler_params=_cparams(("arbitrary",)),
        name=name,
    )(m_pair[0], m_pair[1], x_pair[0], x_pair[1], w_post.reshape(1, D_MODEL), w2.reshape(1, D_MODEL))


def kernel(x_prompt, x_sample, cache_win_k, cache_win_v, state_conv, state_ssm, norm_mix_pre, norm_mix_post,
           w_in, attn_sinks, w_attn_branch, conv_w, conv_b, dt_bias, a_log, d_skip, ssm_norm, w_ssm_branch,
           w_out, norm_ffn_pre, norm_ffn_post, w_gate_up, w_down):
    assert x_prompt.shape == (BATCH, SEQ, D_MODEL) and x_sample.shape == (DEC_BATCH, 1, D_MODEL)
    assert w_in.shape[0] == 1, "one trunk layer"
    cw = cache_win_k.shape[2]
    xp = x_prompt.reshape(NP, D_MODEL)
    xs = x_sample.reshape(NS, D_MODEL)
    w_in0 = w_in[0]
    w_dt = w_in0[:, OFF_DT:OFF_GATES]
    w_gates = w_in0[:, OFF_GATES:]

    hn_p, hn_s, dtg, dtt, dt_s = _prenorm(xp, xs, norm_mix_pre[0], w_dt, dt_bias[0])

    first = lambda rs, rows, cols: [rs[0]]
    (proj,) = _rows_matmul("in_proj", [(hn_p, hn_s)], [(w_in0, 0)], [(0, 0)], first, [(bf16, f32)],
                           tn=1024, n_panels=N_MAIN // 1024)
    proj_p, proj_s = proj

    cos_p, sin_p = _rope_tables(jnp.arange(SEQ, dtype=jnp.int32))
    attn_p, win_k_p = _attn_prompt(proj_p, attn_sinks[0], cos_p, sin_p, _band_bias())
    cos_s, sin_s = _rope_tables(jnp.full((1,), PAST_LEN, dtype=jnp.int32))
    attn_s, win_k_s, win_v_s = _attn_sample(
        proj_s[:, OFF_Q:OFF_K], proj_s[:, OFF_K:OFF_V], proj_s[:, OFF_V:OFF_Z],
        cache_win_k[0].reshape(NS, cw, KV_WIDTH), cache_win_v[0].reshape(NS, cw, KV_WIDTH),
        cos_s, sin_s, attn_sinks[0])

    y_p, ssm_p = _ssd_prompt(proj_p, dtg, dtt, conv_w[0], conv_b[0], a_log[0], d_skip[0], ssm_norm[0])
    y_s, conv_s, ssm_s = _ssd_sample(proj_s[:, OFF_XBC:OFF_DT], proj_s[:, OFF_Z:OFF_XBC], dt_s,
                                     state_conv[0], state_ssm[0], conv_w[0], conv_b[0], a_log[0], d_skip[0],
                                     ssm_norm[0])
    y_s = y_s.reshape(NS, D_INNER)

    (gates,) = _rows_matmul("gates", [(hn_p, hn_s)], [(w_gates, 0)], [(0, 0)],
                            lambda rs, rows, cols: [_sigmoid(rs[0])], [(bf16, f32)],
                            tn=1024, n_panels=2 * D_MODEL // 1024)
    (ssm_d,) = _rows_matmul("ssm_branch", [(y_p, y_s)], [(w_ssm_branch[0], 0)], [(0, 0)],
                            lambda rs, rows, cols: [rows[0] * rs[0]], [(bf16, f32)],
                            tn=512, n_panels=D_MODEL // 512, row_pairs=[gates + (D_MODEL // 512,)])
    (merged,) = _rows_matmul("attn_branch_merge", [(attn_p, attn_s)], [(w_attn_branch[0], 0)], [(0, 0)],
                             lambda rs, rows, cols: [rows[0] * rs[0] + rows[1]], [(bf16, f32)],
                             tn=1024, n_panels=D_MODEL // 1024, row_pairs=[gates + (0,), ssm_d + (0,)])
    (mix,) = _rows_matmul("out_proj", [merged], [(w_out[0], 0)], [(0, 0)], first, [(bf16, f32)],
                          tn=1024, n_panels=D_MODEL // 1024)
    h_p, h_s, hn2_p, hn2_s = _resid_norm("mix_residual", mix, (xp, xs), norm_mix_post[0], norm_ffn_pre[0])

    (act,) = _rows_matmul("ffn_gate_up", [(hn2_p, hn2_s)], [(w_gate_up[0], 0), (w_gate_up[0], D_FF // 512)],
                          [(0, 0), (0, 1)], lambda rs, rows, cols: [_silu(rs[0]) * rs[1]], [(bf16, f32)],
                          tn=512, n_panels=D_FF // 512)
    (ffn,) = _rows_matmul("ffn_down", [act], [(w_down[0], 0)], [(0, 0)], first, [(bf16, f32)],
                          tn=256, n_panels=D_MODEL // 256)
    out_p, out_s = _resid_norm("ffn_residual", ffn, (h_p, h_s), norm_ffn_post[0], None)

    proj_b = proj_p.reshape(BATCH, SEQ, N_MAIN)
    win_v_p = proj_b[:, SEQ - WINDOW:, OFF_V:OFF_Z].astype(f32)
    conv_p = proj_b[:, SEQ - (CONV_W - 1):, OFF_XBC:OFF_DT].astype(f32)
    return (out_p.reshape(BATCH, SEQ, D_MODEL),
            out_s.reshape(DEC_BATCH, 1, D_MODEL),
            win_k_p.reshape(1, BATCH, WINDOW, N_KV_HEADS, HEAD_DIM),
            win_v_p.reshape(1, BATCH, WINDOW, N_KV_HEADS, HEAD_DIM),
            conv_p.reshape(1, BATCH, CONV_W - 1, CONV_DIM),
            ssm_p.reshape(1, BATCH, N_SSM_HEADS, SSM_HEAD_DIM, D_STATE),
            win_k_s.reshape(1, NS, cw, N_KV_HEADS, HEAD_DIM),
            win_v_s.reshape(1, NS, cw, N_KV_HEADS, HEAD_DIM),
            conv_s.reshape(1, NS, CONV_W - 1, CONV_DIM),
            ssm_s.reshape(1, NS, N_SSM_HEADS, SSM_HEAD_DIM, D_STATE))
```

```python
import functools

import numpy as np
import jax
import jax.numpy as jnp
from jax import lax
from jax.experimental import pallas as pl
from jax.experimental.pallas import tpu as pltpu

f32 = jnp.float32
bf16 = jnp.bfloat16

D_MODEL = 2048
BATCH = 4
SEQ = 2048
DEC_BATCH = 128
PAST_LEN = 16384
N_HEADS = 32
N_KV_HEADS = 8
HEAD_DIM = 64
GQA_GROUP = 4
WINDOW = 128
ROPE_THETA = 10000.0
ATTN_WIDTH = N_HEADS * HEAD_DIM
KV_WIDTH = N_KV_HEADS * HEAD_DIM
D_INNER = 4096
SSM_HEAD_DIM = 64
N_SSM_HEADS = 64
D_STATE = 128
N_SSM_GROUPS = 8
HEADS_PER_GROUP = 8
GROUP_WIDTH = HEADS_PER_GROUP * SSM_HEAD_DIM
CONV_W = 4
CONV_DIM = D_INNER + 2 * N_SSM_GROUPS * D_STATE
CHUNK = 128
D_FF = 5632
EPS = 1e-6

OFF_Q = 0
OFF_K = ATTN_WIDTH
OFF_V = OFF_K + KV_WIDTH
OFF_Z = OFF_V + KV_WIDTH
OFF_XBC = OFF_Z + D_INNER
OFF_DT = OFF_XBC + CONV_DIM
OFF_GATES = OFF_DT + N_SSM_HEADS
N_MAIN = OFF_DT

NP = BATCH * SEQ
NS = DEC_BATCH
TP = 1024
NTP = NP // TP
TR = 512
NTR = NP // TR
N_CHUNKS = SEQ // CHUNK

VMEM_LIMIT = 56 * 1024 * 1024


def _cparams(sem):
    return pltpu.CompilerParams(dimension_semantics=sem, vmem_limit_bytes=VMEM_LIMIT)


def _split2(v):
    hi = v.astype(bf16)
    lo = (v - hi.astype(f32)).astype(bf16)
    return hi, lo


def _split3(v):
    hi = v.astype(bf16)
    r = v - hi.astype(f32)
    mid = r.astype(bf16)
    lo = (r - mid.astype(f32)).astype(bf16)
    return hi, mid, lo


def _dot(a, b):
    return jnp.dot(a, b, preferred_element_type=f32)


def _dot_nt(a, b):
    return lax.dot_general(a, b, (((1,), (1,)), ((), ())), preferred_element_type=f32)


def _dot_tn(a, b):
    return lax.dot_general(a, b, (((0,), (0,)), ((), ())), preferred_element_type=f32)


def _dot_f32_lhs(a, b_exact):
    p = _split3(a)
    return _dot(p[0], b_exact) + _dot(p[1], b_exact) + _dot(p[2], b_exact)


def _dot_f32_rhs(a_exact, b):
    p = _split3(b)
    return _dot(a_exact, p[0]) + _dot(a_exact, p[1]) + _dot(a_exact, p[2])


def _silu(x):
    return x * (1.0 / (1.0 + jnp.exp(-x)))


def _sigmoid(x):
    return 1.0 / (1.0 + jnp.exp(-x))


def _softplus(x):
    return jnp.maximum(x, 0.0) + jnp.log1p(jnp.exp(-jnp.abs(x)))


def _rms(x, w):
    return x * lax.rsqrt(jnp.mean(x * x, axis=-1, keepdims=True) + EPS) * w


def _head_selector(n_heads, width):
    r = lax.broadcasted_iota(jnp.int32, (n_heads, n_heads * width), 0)
    c = lax.broadcasted_iota(jnp.int32, (n_heads, n_heads * width), 1)
    return jnp.where(c // width == r, 1.0, 0.0).astype(bf16)


def _prenorm_body(xp_ref, xs_ref, nw_ref, wdt_ref, dtb_ref, hnp_ref, hns_ref, dtg_ref, dtt_ref, dtsg_ref):
    i = pl.program_id(0)
    w_hi, w_lo = _split2(wdt_ref[:, 0:N_SSM_HEADS])

    def dt_rows(hn, grouped_ref):
        a_hi, a_lo = _split2(hn)
        raw = _dot(a_hi, w_hi) + _dot(a_hi, w_lo) + _dot(a_lo, w_hi)
        dt = _softplus(raw + dtb_ref[...])
        for g in range(N_SSM_GROUPS):
            grouped_ref[g] = dt[:, g * HEADS_PER_GROUP:(g + 1) * HEADS_PER_GROUP]
        return dt

    @pl.when(i < NTP)
    def _():
        hn = _rms(xp_ref[...], nw_ref[...])
        hnp_ref[...] = hn.astype(bf16)
        dtt_ref[...] = dt_rows(hn, dtg_ref).T

    @pl.when(i == NTP)
    def _():
        hn = _rms(xs_ref[...], nw_ref[...])
        hns_ref[...] = hn
        dt_rows(hn, dtsg_ref)


def _prenorm(xp, xs, norm_w, w_in, dt_bias):
    pidx = lambda i: (jnp.minimum(i, NTP - 1), 0)
    const = lambda i: (0, 0)
    return pl.pallas_call(
        _prenorm_body,
        grid=(NTP + 1,),
        in_specs=[
            pl.BlockSpec((TP, D_MODEL), pidx),
            pl.BlockSpec((NS, D_MODEL), const),
            pl.BlockSpec((1, D_MODEL), const),
            pl.BlockSpec((D_MODEL, 128), lambda i: (0, OFF_DT // 128)),
            pl.BlockSpec((1, N_SSM_HEADS), const),
        ],
        out_specs=[
            pl.BlockSpec((TP, D_MODEL), pidx),
            pl.BlockSpec((NS, D_MODEL), const),
            pl.BlockSpec((N_SSM_GROUPS, TP, HEADS_PER_GROUP), lambda i: (0, jnp.minimum(i, NTP - 1), 0)),
            pl.BlockSpec((N_SSM_HEADS, TP), lambda i: (0, jnp.minimum(i, NTP - 1))),
            pl.BlockSpec((N_SSM_GROUPS, NS, HEADS_PER_GROUP), lambda i: (0, 0, 0)),
        ],
        out_shape=[
            jax.ShapeDtypeStruct((NP, D_MODEL), bf16),
            jax.ShapeDtypeStruct((NS, D_MODEL), f32),
            jax.ShapeDtypeStruct((N_SSM_GROUPS, NP, HEADS_PER_GROUP), f32),
            jax.ShapeDtypeStruct((N_SSM_HEADS, NP), f32),
            jax.ShapeDtypeStruct((N_SSM_GROUPS, NS, HEADS_PER_GROUP), f32),
        ],
        compiler_params=_cparams(("arbitrary",)),
        name="prenorm_dt",
    )(xp, xs, norm_w.reshape(1, D_MODEL), w_in, dt_bias.reshape(1, -1))


def _rows_matmul_body(*refs, n_a, n_w, n_row, n_col, n_out, dots, epilogue, cast_rows, shifts):
    pos = 0
    a_p = refs[pos:pos + n_a]; pos += n_a
    a_s = refs[pos:pos + n_a]; pos += n_a
    w = refs[pos:pos + n_w]; pos += n_w
    n_tail = sum(1 for s in shifts if s)
    tails = iter(refs[pos:pos + n_tail]); pos += n_tail
    w_tail = [next(tails) if s else None for s in shifts]
    row_p = refs[pos:pos + n_row]; pos += n_row
    row_s = refs[pos:pos + n_row]; pos += n_row
    col = refs[pos:pos + n_col]; pos += n_col
    out_p = refs[pos:pos + n_out]; pos += n_out
    out_s = refs[pos:pos + n_out]; pos += n_out
    wb = refs[pos:pos + n_w]
    i = pl.program_id(1)

    @pl.when(i == 0)
    def _():
        for k in range(n_w):
            rows = w[k].shape[0]

            def cast(c, carry, k=k):
                r0 = pl.multiple_of(c * cast_rows, cast_rows)
                blk = w[k][pl.ds(r0, cast_rows), :]
                if shifts[k]:
                    blk = jnp.concatenate([blk[:, shifts[k]:], w_tail[k][pl.ds(r0, cast_rows), 0:shifts[k]]], axis=1)
                wb[k][pl.ds(r0, cast_rows), :] = blk.astype(bf16)
                return carry

            lax.fori_loop(0, rows // cast_rows, cast, 0)

    def compute(a, row, out):
        rs = [_dot(a[ai][...].astype(bf16), wb[wi][...]) for ai, wi in dots]
        res = epilogue(rs, [r[...].astype(f32) for r in row], [c[...] for c in col])
        for o, v in zip(out, res):
            o[...] = v.astype(o.dtype)

    @pl.when(i == 0)
    def _():
        compute(a_s, row_s, out_s)

    @pl.when(i > 0)
    def _():
        compute(a_p, row_p, out_p)


def _rows_matmul(name, a_pairs, weights, dots, epilogue, out_dtypes, tn, n_panels,
                 row_pairs=(), cols=(), tp=TP, single_buffer_weights=False):
    n_a, n_w, n_row, n_col, n_out = len(a_pairs), len(weights), len(row_pairs), len(cols), len(out_dtypes)
    ptile = lambda i: jnp.maximum(i - 1, 0)
    pidx = lambda n, i: (ptile(i), 0)
    sidx = lambda n, i: (0, 0)
    in_specs, args = [], []
    for ap, _ in a_pairs:
        in_specs.append(pl.BlockSpec((tp, ap.shape[1]), pidx)); args.append(ap)
    for _, as_ in a_pairs:
        in_specs.append(pl.BlockSpec((NS, as_.shape[1]), sidx)); args.append(as_)
    w_mode = dict(pipeline_mode=pl.Buffered(1)) if single_buffer_weights else {}
    for wk, off, _ in weights:
        in_specs.append(pl.BlockSpec((wk.shape[0], tn), functools.partial(lambda n, i, off: (0, n + off), off=off),
                                     **w_mode))
        args.append(wk)
    for wk, off, shift in weights:
        if shift:
            in_specs.append(pl.BlockSpec((wk.shape[0], 128), functools.partial(
                lambda n, i, off: (0, (n + off + 1) * (tn // 128)), off=off), **w_mode))
            args.append(wk)
    out_p_idx = lambda n, i: (ptile(i), n)
    out_s_idx = lambda n, i: (0, n)
    for rp, _, off in row_pairs:
        in_specs.append(pl.BlockSpec((tp, tn), functools.partial(lambda n, i, off: (ptile(i), n + off), off=off)))
        args.append(rp)
    for _, rs, off in row_pairs:
        in_specs.append(pl.BlockSpec((NS, tn), functools.partial(lambda n, i, off: (0, n + off), off=off)))
        args.append(rs)
    for c in cols:
        in_specs.append(pl.BlockSpec((1, tn), lambda n, i: (0, n))); args.append(c)
    width = n_panels * tn
    out_specs = ([pl.BlockSpec((tp, tn), out_p_idx)] * n_out + [pl.BlockSpec((NS, tn), out_s_idx)] * n_out)
    out_shape = ([jax.ShapeDtypeStruct((NP, width), dp) for dp, _ in out_dtypes]
                 + [jax.ShapeDtypeStruct((NS, width), ds) for _, ds in out_dtypes])
    scratch = [pltpu.VMEM((wk.shape[0], tn), bf16) for wk, _, _ in weights]
    body = functools.partial(_rows_matmul_body, n_a=n_a, n_w=n_w, n_row=n_row, n_col=n_col, n_out=n_out,
                             dots=tuple(dots), epilogue=epilogue, cast_rows=256,
                             shifts=tuple(s for _, _, s in weights))
    res = pl.pallas_call(
        body,
        grid=(n_panels, NP // tp + 1),
        in_specs=in_specs,
        out_specs=out_specs,
        out_shape=out_shape,
        scratch_shapes=scratch,
        compiler_params=_cparams(("arbitrary", "arbitrary")),
        name=name,
    )(*args)
    return [(res[k], res[n_out + k]) for k in range(n_out)]


def _rope_tables(pos):
    half = HEAD_DIM // 2
    inv = ROPE_THETA ** (-jnp.arange(half, dtype=f32) / half)
    ang = pos.astype(f32)[:, None] * inv[None, :]
    cos, sin = jnp.cos(ang), jnp.sin(ang)
    return jnp.concatenate([cos, cos, cos, cos], axis=1), jnp.concatenate([-sin, sin, -sin, sin], axis=1)


def _rope_chunk(x, cos, sin, first_half):
    from_right = pltpu.roll(x, 96, 1)
    from_left = pltpu.roll(x, 32, 1)
    return x * cos + jnp.where(first_half, from_right, from_left) * sin


def _rope(x, cos, sin):
    rows, width = x.shape
    lane = lax.broadcasted_iota(jnp.int32, (rows, 128), 1)
    first_half = (lane % HEAD_DIM) < (HEAD_DIM // 2)
    return [_rope_chunk(x[:, c * 128:(c + 1) * 128], cos, sin, first_half) for c in range(width // 128)]


def _attn_prompt_body(sink_ref, q_ref, kv_ref, cos_ref, sin_ref, cost_ref, sint_ref, bias_ref,
                      o_ref, wk_ref, kbuf, vbuf_t):
    n = pl.program_id(1)
    cur = n % 2
    prev = 1 - cur

    @pl.when(n == 0)
    def _():
        kbuf[1] = jnp.zeros((WINDOW, KV_WIDTH), bf16)
        vbuf_t[1] = jnp.zeros((KV_WIDTH, WINDOW), bf16)

    kc = _rope(kv_ref[:, 0:KV_WIDTH].astype(f32), cos_ref[...], sin_ref[...])
    for c, v in enumerate(kc):
        kbuf[cur, :, c * 128:(c + 1) * 128] = v.astype(bf16)
    vbuf_t[cur] = kv_ref[:, KV_WIDTH:2 * KV_WIDTH].astype(f32).T.astype(bf16)

    @pl.when(n == N_CHUNKS - 1)
    def _():
        for c, v in enumerate(kc):
            wk_ref[:, c * 128:(c + 1) * 128] = v

    qt = q_ref[...].astype(f32).T
    cost, sint = cost_ref[...], sint_ref[...]
    bias = bias_ref[...]
    scale = HEAD_DIM ** -0.5
    half = HEAD_DIM // 2
    pad = jnp.zeros((HEAD_DIM, CHUNK), f32)
    heads_per_pair = 2 * GQA_GROUP
    heads = []
    for pr in range(N_KV_HEADS // 2):
        qcols = []
        for hh in range(heads_per_pair):
            h = pr * heads_per_pair + hh
            x1 = qt[h * HEAD_DIM:h * HEAD_DIM + half, :]
            x2 = qt[h * HEAD_DIM + half:(h + 1) * HEAD_DIM, :]
            r1 = (x1 * cost - x2 * sint) * scale
            r2 = (x2 * cost + x1 * sint) * scale
            qcols.append(jnp.concatenate([r1, r2, pad] if hh < GQA_GROUP else [pad, r1, r2], axis=0))
        qw = jnp.concatenate(qcols, axis=1).astype(bf16)
        lanes = slice(pr * 128, (pr + 1) * 128)
        kpair = jnp.concatenate([kbuf[prev, :, lanes], kbuf[cur, :, lanes]], axis=0)
        st = _dot(kpair, qw)
        st = st + jnp.concatenate([bias] * heads_per_pair, axis=1)
        sink = sink_ref[:, pr * heads_per_pair * CHUNK:(pr + 1) * heads_per_pair * CHUNK]
        m = jnp.maximum(jnp.max(st, axis=0, keepdims=True), sink)
        p = jnp.exp(st - m)
        inv = 1.0 / (jnp.sum(p, axis=0, keepdims=True) + jnp.exp(sink - m))
        pb = p.astype(bf16)
        for k in range(2):
            g = 2 * pr + k
            cols = slice(k * GQA_GROUP * CHUNK, (k + 1) * GQA_GROUP * CHUNK)
            dims = slice(g * HEAD_DIM, (g + 1) * HEAD_DIM)
            vgt = jnp.concatenate([vbuf_t[prev, dims, :], vbuf_t[cur, dims, :]], axis=1)
            og = _dot(vgt, pb[:, cols]) * inv[:, cols]
            heads += [og[:, j * CHUNK:(j + 1) * CHUNK] for j in range(GQA_GROUP)]
    o_ref[...] = jnp.concatenate(heads, axis=0).T.astype(o_ref.dtype)


def _band_bias():
    qi = np.arange(WINDOW)[None, :]
    kj = np.arange(2 * WINDOW)[:, None]
    cur = (kj >= WINDOW) & (kj - WINDOW <= qi)
    prev = (kj < WINDOW) & (kj > qi)
    neg = np.float32(-np.inf)
    b0 = np.where(cur, np.float32(0), neg)
    b1 = np.where(cur | prev, np.float32(0), neg)
    return jnp.asarray(np.stack([b0, b1]).astype(np.float32))


def _attn_prompt(proj_p, sinks):
    pos = jnp.arange(SEQ, dtype=jnp.int32)
    cos, sin = _rope_tables(pos)
    half = HEAD_DIM // 2
    cos_t, sin_t = cos[:, :half].T, sin[:, half:HEAD_DIM].T
    row = lambda b, n: b * N_CHUNKS + n
    return pl.pallas_call(
        _attn_prompt_body,
        grid=(BATCH, N_CHUNKS),
        in_specs=[
            pl.BlockSpec((1, N_HEADS * CHUNK), lambda b, n: (0, 0)),
            pl.BlockSpec((CHUNK, ATTN_WIDTH), lambda b, n: (row(b, n), 0)),
            pl.BlockSpec((CHUNK, 2 * KV_WIDTH), lambda b, n: (row(b, n), OFF_K // (2 * KV_WIDTH))),
            pl.BlockSpec((CHUNK, 128), lambda b, n: (n, 0)),
            pl.BlockSpec((CHUNK, 128), lambda b, n: (n, 0)),
            pl.BlockSpec((half, CHUNK), lambda b, n: (0, n)),
            pl.BlockSpec((half, CHUNK), lambda b, n: (0, n)),
            pl.BlockSpec((None, 2 * WINDOW, CHUNK), lambda b, n: (jnp.minimum(n, 1), 0, 0)),
        ],
        out_specs=[
            pl.BlockSpec((CHUNK, ATTN_WIDTH), lambda b, n: (row(b, n), 0)),
            pl.BlockSpec((None, WINDOW, KV_WIDTH), lambda b, n: (b, 0, 0)),
        ],
        out_shape=[
            jax.ShapeDtypeStruct((NP, ATTN_WIDTH), bf16),
            jax.ShapeDtypeStruct((BATCH, WINDOW, KV_WIDTH), f32),
        ],
        scratch_shapes=[pltpu.VMEM((2, WINDOW, KV_WIDTH), bf16), pltpu.VMEM((2, KV_WIDTH, WINDOW), bf16)],
        compiler_params=_cparams(("arbitrary", "arbitrary")),
        name="attn_prompt",
    )(jnp.repeat(sinks, CHUNK).reshape(1, N_HEADS * CHUNK), proj_p, proj_p, cos, sin, cos_t, sin_t, _band_bias())


SB = 8


def _dup_halves(x, lane_first):
    swapped = pltpu.roll(x, 64, x.ndim - 1)
    return jnp.where(lane_first, x, swapped), jnp.where(lane_first, swapped, x)


def _attn_sample_body(q_ref, k_ref, v_ref, ck_ref, cv_ref, cos_ref, sin_ref, sink_ref, e_ref, et_ref,
                      o_ref, wk_ref, wv_ref):
    cos, sin = cos_ref[...], sin_ref[...]
    scale = HEAD_DIM ** -0.5
    qc = [c * scale for c in _rope(q_ref[...], cos, sin)]
    kc = _rope(k_ref[...], cos, sin)
    vn = v_ref[...]
    cw = ck_ref.shape[1]
    lane3 = lax.broadcasted_iota(jnp.int32, (SB, cw, 128), 2) < HEAD_DIM
    lane2 = lax.broadcasted_iota(jnp.int32, (SB, 128), 1) < HEAD_DIM

    s = jnp.zeros((SB * cw, 128), f32)
    sn = jnp.zeros((SB, 128), f32)
    for v in range(KV_WIDTH // 128):
        kd = _dup_halves(ck_ref[:, :, v * 128:(v + 1) * 128], lane3)
        knd = _dup_halves(kc[v], lane2)
        for half in range(2):
            for t in range(2):
                c = 4 * v + 2 * half + t
                e_c = e_ref[c * 128:(c + 1) * 128, :]
                prod = (kd[half] * qc[c][:, None, :]).astype(bf16).reshape(SB * cw, 128)
                s = s + _dot(prod, e_c)
                sn = sn + _dot((knd[half] * qc[c]).astype(bf16), e_c)
    s = s.reshape(SB, cw, 128)
    key = lax.broadcasted_iota(jnp.int32, (SB, cw, 128), 1)
    s = jnp.where(key == 0, -jnp.inf, s)
    sink = sink_ref[...]
    m = jnp.maximum(jnp.maximum(jnp.max(s, axis=1), sn), sink)
    p = jnp.exp(s - m[:, None, :])
    pn = jnp.exp(sn - m)
    den = jnp.sum(p, axis=1) + pn + jnp.exp(sink - m)
    inv = 1.0 / den
    p = (p * inv[:, None, :]).astype(bf16).reshape(SB * cw, 128)
    pn = (pn * inv).astype(bf16)

    for v in range(KV_WIDTH // 128):
        vd = _dup_halves(cv_ref[:, :, v * 128:(v + 1) * 128], lane3)
        vnd = _dup_halves(vn[:, v * 128:(v + 1) * 128], lane2)
        for half in range(2):
            for t in range(2):
                c = 4 * v + 2 * half + t
                et_c = et_ref[:, c * 128:(c + 1) * 128]
                pe = _dot(p, et_c).reshape(SB, cw, 128)
                pne = _dot(pn, et_c)
                o_ref[:, c * 128:(c + 1) * 128] = jnp.sum(pe * vd[half], axis=1) + pne * vnd[half]

    wk_ref[:, 0:cw - 1, :] = ck_ref[:, 1:cw, :]
    wv_ref[:, 0:cw - 1, :] = cv_ref[:, 1:cw, :]
    for c, val in enumerate(kc):
        wk_ref[:, cw - 1:cw, c * 128:(c + 1) * 128] = val[:, None, :]
    wv_ref[:, cw - 1:cw, :] = vn[:, None, :]


def _attn_sample(q, k, v, cache_k, cache_v, cos, sin, sinks):
    cw = cache_k.shape[1]
    sel = np.zeros((ATTN_WIDTH, 128), np.float32)
    sel[np.arange(ATTN_WIDTH), np.arange(ATTN_WIDTH) // HEAD_DIM] = 1.0
    e = jnp.asarray(sel, dtype=bf16)
    et = jnp.asarray(sel.T, dtype=bf16)
    sink_row = jnp.zeros((1, 128), f32).at[0, :N_HEADS].set(sinks)
    rows = lambda i: (i, 0)
    rows3 = lambda i: (i, 0, 0)
    const = lambda i: (0, 0)
    return pl.pallas_call(
        _attn_sample_body,
        grid=(NS // SB,),
        in_specs=[
            pl.BlockSpec((SB, ATTN_WIDTH), rows),
            pl.BlockSpec((SB, KV_WIDTH), rows),
            pl.BlockSpec((SB, KV_WIDTH), rows),
            pl.BlockSpec((SB, cw, KV_WIDTH), rows3),
            pl.BlockSpec((SB, cw, KV_WIDTH), rows3),
            pl.BlockSpec((1, 128), const),
            pl.BlockSpec((1, 128), const),
            pl.BlockSpec((1, 128), const),
            pl.BlockSpec((ATTN_WIDTH, 128), const),
            pl.BlockSpec((128, ATTN_WIDTH), const),
        ],
        out_specs=[
            pl.BlockSpec((SB, ATTN_WIDTH), rows),
            pl.BlockSpec((SB, cw, KV_WIDTH), rows3),
            pl.BlockSpec((SB, cw, KV_WIDTH), rows3),
        ],
        out_shape=[
            jax.ShapeDtypeStruct((NS, ATTN_WIDTH), f32),
            jax.ShapeDtypeStruct((NS, cw, KV_WIDTH), f32),
            jax.ShapeDtypeStruct((NS, cw, KV_WIDTH), f32),
        ],
        compiler_params=_cparams(("arbitrary",)),
        name="attn_sample",
    )(q, k, v, cache_k, cache_v, cos, sin, sink_row, e, et)


GPS = 2
XW = GPS * GROUP_WIDTH
BW = GPS * D_STATE
XBCW = XW + 2 * BW
TAIL = 16


def _ssd_prompt_body(x_ref, b_ref, c_ref, z_ref, dt_ref, dtt_ref, cwx_ref, cwb_ref, cwc_ref,
                     cbx_ref, cbb_ref, cbc_ref, alogt_ref, dsk_ref, nw_ref,
                     y_ref, st_ref, hst, tails):
    c = pl.program_id(2)
    cur = c % 2
    prev = 1 - cur

    @pl.when(c == 0)
    def _():
        hst[1] = jnp.zeros((GPS, D_STATE, GROUP_WIDTH), f32)
        tails[1] = jnp.zeros((TAIL, XBCW), bf16)

    xbc = jnp.concatenate([x_ref[...], b_ref[...], c_ref[...]], axis=1)
    stacked = jnp.concatenate([jnp.zeros((CHUNK - TAIL, XBCW), bf16), tails[prev], xbc], axis=0)
    tails[cur] = xbc[CHUNK - TAIL:, :]
    ti = lax.broadcasted_iota(jnp.int32, (CONV_W * CHUNK, 2 * CHUNK), 0)
    si = lax.broadcasted_iota(jnp.int32, (CONV_W * CHUNK, 2 * CHUNK), 1)
    pick = jnp.where(si == CHUNK + (ti % CHUNK) - (ti // CHUNK), 1.0, 0.0).astype(bf16)
    taps = _dot(pick, stacked)
    w = jnp.concatenate([cwx_ref[...], cwb_ref[...], cwc_ref[...]], axis=1)
    acc = jnp.concatenate([cbx_ref[...], cbb_ref[...], cbc_ref[...]], axis=1)
    for j in range(CONV_W):
        acc = acc + w[CONV_W - 1 - j:CONV_W - j, :] * taps[j * CHUNK:(j + 1) * CHUNK, :]
    act = _silu(acc)

    li = lax.broadcasted_iota(jnp.int32, (CHUNK, CHUNK), 0)
    si = lax.broadcasted_iota(jnp.int32, (CHUNK, CHUNK), 1)
    causal = li >= si
    trit = jnp.where(li <= si, 1.0, 0.0).astype(bf16)
    sel = _head_selector(HEADS_PER_GROUP, SSM_HEAD_DIM)

    def expand(v):
        hi = v.astype(bf16)
        lo = (v - hi.astype(f32)).astype(bf16)
        return _dot(hi, sel) + _dot(lo, sel)

    expand_row = lambda v: _dot_f32_lhs(jnp.broadcast_to(v, (8, HEADS_PER_GROUP)), sel)[0:1, :]
    lane = lax.broadcasted_iota(jnp.int32, (CHUNK, 128), 1)
    zero = jnp.zeros((CHUNK, 128), bf16)
    log2e = 1.4426950408889634

    for k in range(GPS):
        xs = act[:, k * GROUP_WIDTH:(k + 1) * GROUP_WIDTH]
        bmb = act[:, XW + k * D_STATE:XW + (k + 1) * D_STATE].astype(bf16)
        cmb = act[:, XW + BW + k * D_STATE:XW + BW + (k + 1) * D_STATE].astype(bf16)
        dtt = dtt_ref[k * HEADS_PER_GROUP:(k + 1) * HEADS_PER_GROUP, :]
        dat = dtt * (-jnp.exp(alogt_ref[k]))
        acumt = _dot_f32_lhs(dat, trit)
        acum = acumt.T
        a_last = acum[CHUNK - 1:CHUNK, :]
        acum2, acumt2 = acum * log2e, acumt * log2e

        xb = xs.astype(bf16)
        xdec = (xs * expand(dt_ref[k] * jnp.exp(a_last - acum))).astype(bf16)

        cb = _dot_nt(cmb, bmb)
        pairs = []
        for pr in range(HEADS_PER_GROUP // 2):
            ms = []
            for r in (2 * pr, 2 * pr + 1):
                seg = acum2[:, r:r + 1] - acumt2[r:r + 1, :]
                decay = jnp.exp2(jnp.where(causal, seg, -jnp.inf))
                ms.append((cb * decay * dtt[r:r + 1, :]).astype(bf16))
            xp = xb[:, pr * 128:(pr + 1) * 128]
            x0 = jnp.where(lane < SSM_HEAD_DIM, xp, zero)
            x1 = jnp.where(lane < SSM_HEAD_DIM, zero, xp)
            pairs.append(_dot(jnp.concatenate(ms, axis=1), jnp.concatenate([x0, x1], axis=0)))
        y = jnp.concatenate(pairs, axis=1)

        h_prev = hst[prev, k]
        y = y + _dot(cmb, h_prev.astype(bf16)) * expand(jnp.exp2(acum2))
        y = y + expand_row(dsk_ref[k]) * xs

        cols = slice(k * GROUP_WIDTH, (k + 1) * GROUP_WIDTH)
        gz = y * _silu(z_ref[:, cols].astype(f32))
        y_ref[:, cols] = _rms(gz, nw_ref[:, cols]).astype(y_ref.dtype)

        h_new = h_prev * expand_row(jnp.exp(a_last)) + _dot_tn(bmb, xdec)
        hst[cur, k] = h_new

    @pl.when(c == N_CHUNKS - 1)
    def _():
        for k in range(GPS):
            st_ref[k] = hst[cur, k].T


def _ssd_prompt(proj_p, dtg, dtt, conv_w, conv_b, a_log, d_skip, ssm_norm):
    row = lambda b, g, c: b * N_CHUNKS + c
    xoff = OFF_XBC // XW
    boff = (OFF_XBC + D_INNER) // BW
    coff = boff + N_SSM_GROUPS // GPS
    zoff = OFF_Z // XW
    wboff = D_INNER // BW
    wcoff = wboff + N_SSM_GROUPS // GPS
    alog_t = a_log.reshape(N_SSM_GROUPS, HEADS_PER_GROUP, 1)
    dsk_g = d_skip.reshape(N_SSM_GROUPS, 1, HEADS_PER_GROUP)
    conv_b = conv_b.reshape(1, CONV_DIM)
    grp = lambda b, g, c: (g, 0, 0)
    return pl.pallas_call(
        _ssd_prompt_body,
        grid=(BATCH, N_SSM_GROUPS // GPS, N_CHUNKS),
        in_specs=[
            pl.BlockSpec((CHUNK, XW), lambda b, g, c: (row(b, g, c), xoff + g)),
            pl.BlockSpec((CHUNK, BW), lambda b, g, c: (row(b, g, c), boff + g)),
            pl.BlockSpec((CHUNK, BW), lambda b, g, c: (row(b, g, c), coff + g)),
            pl.BlockSpec((CHUNK, XW), lambda b, g, c: (row(b, g, c), zoff + g)),
            pl.BlockSpec((GPS, CHUNK, HEADS_PER_GROUP), lambda b, g, c: (g, row(b, g, c), 0)),
            pl.BlockSpec((GPS * HEADS_PER_GROUP, CHUNK), lambda b, g, c: (g, row(b, g, c))),
            pl.BlockSpec((CONV_W, XW), lambda b, g, c: (0, g)),
            pl.BlockSpec((CONV_W, BW), lambda b, g, c: (0, wboff + g)),
            pl.BlockSpec((CONV_W, BW), lambda b, g, c: (0, wcoff + g)),
            pl.BlockSpec((1, XW), lambda b, g, c: (0, g)),
            pl.BlockSpec((1, BW), lambda b, g, c: (0, wboff + g)),
            pl.BlockSpec((1, BW), lambda b, g, c: (0, wcoff + g)),
            pl.BlockSpec((GPS, HEADS_PER_GROUP, 1), grp),
            pl.BlockSpec((GPS, 1, HEADS_PER_GROUP), grp),
            pl.BlockSpec((1, XW), lambda b, g, c: (0, g)),
        ],
        out_specs=[
            pl.BlockSpec((CHUNK, XW), lambda b, g, c: (row(b, g, c), g)),
            pl.BlockSpec((None, GPS, GROUP_WIDTH, D_STATE), lambda b, g, c: (b, g, 0, 0)),
        ],
        out_shape=[
            jax.ShapeDtypeStruct((NP, D_INNER), bf16),
            jax.ShapeDtypeStruct((BATCH, N_SSM_GROUPS, GROUP_WIDTH, D_STATE), f32),
        ],
        scratch_shapes=[
            pltpu.VMEM((2, GPS, D_STATE, GROUP_WIDTH), f32),
            pltpu.VMEM((2, TAIL, XBCW), bf16),
        ],
        compiler_params=_cparams(("arbitrary", "arbitrary", "arbitrary")),
        name="ssd_prompt",
    )(proj_p, proj_p, proj_p, proj_p, dtg, dtt, conv_w, conv_w, conv_w, conv_b, conv_b, conv_b,
      alog_t, dsk_g, ssm_norm.reshape(1, D_INNER))


SQ = 16
GSS = 2


def _ssd_sample_body(x_ref, b_ref, c_ref, sx_ref, sb_ref, sc_ref, z_ref, dt_ref, h_ref,
                     cwx_ref, cwb_ref, cwc_ref, cbx_ref, cbb_ref, cbc_ref, alog_ref, dsk_ref, nw_ref,
                     y_ref, ho_ref):
    def conv(new_ref, st_ref, w_ref, bias_ref):
        w = w_ref[...]
        acc = bias_ref[...] + w[CONV_W - 1:CONV_W, :] * new_ref[...]
        for j in range(CONV_W - 1):
            acc = acc + w[j:j + 1, :] * st_ref[:, j, :]
        return _silu(acc)

    xs_all = conv(x_ref, sx_ref, cwx_ref, cbx_ref)
    bm_all = conv(b_ref, sb_ref, cwb_ref, cbb_ref)
    cm_all = conv(c_ref, sc_ref, cwc_ref, cbc_ref)
    zs_all = _silu(z_ref[...])
    sel = _head_selector(HEADS_PER_GROUP, SSM_HEAD_DIM)
    exact = lambda v: v.astype(bf16).astype(f32)
    zeros = jnp.zeros((SQ, D_STATE), f32)
    ones = jnp.ones((SQ, D_STATE), f32)
    wide = (6 * SQ, SQ * 2 * D_STATE)
    row_seq = lax.broadcasted_iota(jnp.int32, wide, 0) % SQ
    col_seq = lax.broadcasted_iota(jnp.int32, wide, 1) // (2 * D_STATE)
    c_row = lax.broadcasted_iota(jnp.int32, (SQ, SQ * D_STATE), 0)
    c_col = lax.broadcasted_iota(jnp.int32, (SQ, SQ * D_STATE), 1) // D_STATE

    for k in range(GSS):
        xs = xs_all[:, k * GROUP_WIDTH:(k + 1) * GROUP_WIDTH]
        bm = bm_all[:, k * D_STATE:(k + 1) * D_STATE]
        cm = cm_all[:, k * D_STATE:(k + 1) * D_STATE]
        dt = dt_ref[k]
        decay = jnp.exp(dt * (-jnp.exp(alog_ref[k])))
        xdt = xs * _dot_f32_lhs(dt, sel)
        dec = _dot_f32_lhs(decay, sel)

        x_hi = exact(xdt)
        d_hi = exact(dec)
        d_mid = exact(dec - d_hi)
        b_hi = exact(bm)
        lhs = jnp.concatenate([x_hi, x_hi, xdt - x_hi, d_hi, d_mid, dec - d_hi - d_mid], axis=0)
        lhs_t = lhs.T.astype(bf16)
        outer_rows = [jnp.concatenate([v, zeros], axis=1) for v in (b_hi, bm - b_hi, b_hi)]
        rhs = jnp.concatenate(outer_rows + [jnp.concatenate([zeros, ones], axis=1)] * 3, axis=0)
        rhs_wide = jnp.where(row_seq == col_seq, jnp.concatenate([rhs] * SQ, axis=1), 0.0).astype(bf16)
        both = _dot(lhs_t, rhs_wide)
        h_bf = []
        rows = slice(k * GROUP_WIDTH, (k + 1) * GROUP_WIDTH)
        for s in range(SQ):
            upd = both[:, s * 2 * D_STATE:(s + 1) * 2 * D_STATE]
            h_new = h_ref[s, rows, :] * upd[:, D_STATE:] + upd[:, 0:D_STATE]
            ho_ref[s, rows, :] = h_new
            h_bf.append(h_new.astype(bf16))
        c_diag = jnp.where(c_row == c_col, jnp.concatenate([cm] * SQ, axis=1), 0.0).astype(bf16)
        y = _dot_nt(c_diag, jnp.concatenate(h_bf, axis=1))
        dsk = _dot_f32_lhs(jnp.broadcast_to(dsk_ref[k], (SQ, HEADS_PER_GROUP)), sel)
        gz = (y + dsk * xs) * zs_all[:, rows]
        y_ref[:, rows] = _rms(gz, nw_ref[:, rows])


def _ssd_sample(proj_s, dtsg, state_conv, state_ssm, conv_w, conv_b, a_log, d_skip, ssm_norm):
    xw, bw = GSS * GROUP_WIDTH, GSS * D_STATE
    xoff = OFF_XBC // xw
    boff = (OFF_XBC + D_INNER) // bw
    coff = boff + N_SSM_GROUPS // GSS
    zoff = OFF_Z // xw
    wboff = D_INNER // bw
    wcoff = wboff + N_SSM_GROUPS // GSS
    conv_b = conv_b.reshape(1, CONV_DIM)
    grp = lambda i, g: (g, 0, 0)
    return pl.pallas_call(
        _ssd_sample_body,
        grid=(NS // SQ, N_SSM_GROUPS // GSS),
        in_specs=[
            pl.BlockSpec((SQ, xw), lambda i, g: (i, xoff + g)),
            pl.BlockSpec((SQ, bw), lambda i, g: (i, boff + g)),
            pl.BlockSpec((SQ, bw), lambda i, g: (i, coff + g)),
            pl.BlockSpec((SQ, CONV_W - 1, xw), lambda i, g: (i, 0, g)),
            pl.BlockSpec((SQ, CONV_W - 1, bw), lambda i, g: (i, 0, wboff + g)),
            pl.BlockSpec((SQ, CONV_W - 1, bw), lambda i, g: (i, 0, wcoff + g)),
            pl.BlockSpec((SQ, xw), lambda i, g: (i, zoff + g)),
            pl.BlockSpec((GSS, SQ, HEADS_PER_GROUP), lambda i, g: (g, i, 0)),
            pl.BlockSpec((SQ, xw, D_STATE), lambda i, g: (i, g, 0)),
            pl.BlockSpec((CONV_W, xw), lambda i, g: (0, g)),
            pl.BlockSpec((CONV_W, bw), lambda i, g: (0, wboff + g)),
            pl.BlockSpec((CONV_W, bw), lambda i, g: (0, wcoff + g)),
            pl.BlockSpec((1, xw), lambda i, g: (0, g)),
            pl.BlockSpec((1, bw), lambda i, g: (0, wboff + g)),
            pl.BlockSpec((1, bw), lambda i, g: (0, wcoff + g)),
            pl.BlockSpec((GSS, 1, HEADS_PER_GROUP), grp),
            pl.BlockSpec((GSS, 1, HEADS_PER_GROUP), grp),
            pl.BlockSpec((1, xw), lambda i, g: (0, g)),
        ],
        out_specs=[
            pl.BlockSpec((SQ, xw), lambda i, g: (i, g)),
            pl.BlockSpec((SQ, xw, D_STATE), lambda i, g: (i, g, 0)),
        ],
        out_shape=[
            jax.ShapeDtypeStruct((NS, D_INNER), f32),
            jax.ShapeDtypeStruct((NS, D_INNER, D_STATE), f32),
        ],
        compiler_params=_cparams(("arbitrary", "arbitrary")),
        name="ssd_sample",
    )(proj_s, proj_s, proj_s, state_conv, state_conv, state_conv, proj_s, dtsg,
      state_ssm.reshape(NS, D_INNER, D_STATE), conv_w, conv_w, conv_w, conv_b, conv_b, conv_b,
      a_log.reshape(N_SSM_GROUPS, 1, HEADS_PER_GROUP), d_skip.reshape(N_SSM_GROUPS, 1, HEADS_PER_GROUP),
      ssm_norm.reshape(1, D_INNER))


def _resid_norm_body(*refs, n_out):
    mp, ms, xp, xs, w1, w2 = refs[:6]
    outs = refs[6:]
    i = pl.program_id(0)

    def compute(m, x, o):
        h = x[...] + _rms(m[...].astype(f32), w1[...])
        o[0][...] = h
        if n_out == 2:
            o[1][...] = _rms(h, w2[...]).astype(o[1].dtype)

    @pl.when(i < NTR)
    def _():
        compute(mp, xp, outs[0::2] if n_out == 2 else outs[0:1])

    @pl.when(i == NTR)
    def _():
        compute(ms, xs, outs[1::2] if n_out == 2 else outs[1:2])


def _resid_norm(name, m_pair, x_pair, w_post, w_next):
    n_out = 2 if w_next is not None else 1
    pidx = lambda i: (jnp.minimum(i, NTR - 1), 0)
    const = lambda i: (0, 0)
    w2 = w_next if w_next is not None else w_post
    out_specs = [pl.BlockSpec((TR, D_MODEL), pidx), pl.BlockSpec((NS, D_MODEL), const)]
    out_shape = [jax.ShapeDtypeStruct((NP, D_MODEL), f32), jax.ShapeDtypeStruct((NS, D_MODEL), f32)]
    if n_out == 2:
        out_specs += [pl.BlockSpec((TR, D_MODEL), pidx), pl.BlockSpec((NS, D_MODEL), const)]
        out_shape += [jax.ShapeDtypeStruct((NP, D_MODEL), bf16), jax.ShapeDtypeStruct((NS, D_MODEL), f32)]
    return pl.pallas_call(
        functools.partial(_resid_norm_body, n_out=n_out),
        grid=(NTR + 1,),
        in_specs=[
            pl.BlockSpec((TR, D_MODEL), pidx), pl.BlockSpec((NS, D_MODEL), const),
            pl.BlockSpec((TR, D_MODEL), pidx), pl.BlockSpec((NS, D_MODEL), const),
            pl.BlockSpec((1, D_MODEL), const), pl.BlockSpec((1, D_MODEL), const),
        ],
        out_specs=out_specs,
        out_shape=out_shape,
        compiler_params=_cparams(("arbitrary",)),
        name=name,
    )(m_pair[0], m_pair[1], x_pair[0], x_pair[1], w_post.reshape(1, D_MODEL), w2.reshape(1, D_MODEL))


def kernel(x_prompt, x_sample, cache_win_k, cache_win_v, state_conv, state_ssm, norm_mix_pre, norm_mix_post,
           w_in, attn_sinks, w_attn_branch, conv_w, conv_b, dt_bias, a_log, d_skip, ssm_norm, w_ssm_branch,
           w_out, norm_ffn_pre, norm_ffn_post, w_gate_up, w_down):
    assert x_prompt.shape == (BATCH, SEQ, D_MODEL) and x_sample.shape == (DEC_BATCH, 1, D_MODEL)
    assert w_in.shape[0] == 1, "one trunk layer"
    cw = cache_win_k.shape[2]
    xp = x_prompt.reshape(NP, D_MODEL)
    xs = x_sample.reshape(NS, D_MODEL)
    w_in0 = w_in[0]

    hn_p, hn_s, dtg, dtt, dtsg = _prenorm(xp, xs, norm_mix_pre[0], w_in0, dt_bias[0])

    first = lambda rs, rows, cols: [rs[0]]
    (proj,) = _rows_matmul("in_proj", [(hn_p, hn_s)], [(w_in0, 0, 0)], [(0, 0)], first, [(bf16, f32)],
                           tn=1024, n_panels=N_MAIN // 1024)
    proj_p, proj_s = proj

    attn_p, win_k_p = _attn_prompt(proj_p, attn_sinks[0])
    cos_s, sin_s = _rope_tables(jnp.full((1,), PAST_LEN, dtype=jnp.int32))
    attn_s, win_k_s, win_v_s = _attn_sample(
        proj_s[:, OFF_Q:OFF_K], proj_s[:, OFF_K:OFF_V], proj_s[:, OFF_V:OFF_Z],
        cache_win_k[0].reshape(NS, cw, KV_WIDTH), cache_win_v[0].reshape(NS, cw, KV_WIDTH),
        cos_s, sin_s, attn_sinks[0])

    y_p, ssm_p = _ssd_prompt(proj_p, dtg, dtt, conv_w[0], conv_b[0], a_log[0], d_skip[0], ssm_norm[0])
    y_s, ssm_s = _ssd_sample(proj_s, dtsg, state_conv[0], state_ssm[0], conv_w[0], conv_b[0], a_log[0],
                             d_skip[0], ssm_norm[0])

    (gates,) = _rows_matmul("gates", [(hn_p, hn_s)], [(w_in0, OFF_DT // 1024, OFF_GATES - OFF_DT)], [(0, 0)],
                            lambda rs, rows, cols: [_sigmoid(rs[0])], [(bf16, f32)],
                            tn=1024, n_panels=2 * D_MODEL // 1024)
    (ssm_d,) = _rows_matmul("ssm_branch", [(y_p, y_s)], [(w_ssm_branch[0], 0, 0)], [(0, 0)],
                            lambda rs, rows, cols: [rows[0] * rs[0]], [(bf16, f32)],
                            tn=512, n_panels=D_MODEL // 512, row_pairs=[gates + (D_MODEL // 512,)])
    (merged,) = _rows_matmul("attn_branch_merge", [(attn_p, attn_s)], [(w_attn_branch[0], 0, 0)], [(0, 0)],
                             lambda rs, rows, cols: [rows[0] * rs[0] + rows[1]], [(bf16, f32)],
                             tn=1024, n_panels=D_MODEL // 1024, row_pairs=[gates + (0,), ssm_d + (0,)])
    (mix,) = _rows_matmul("out_proj", [merged], [(w_out[0], 0, 0)], [(0, 0)], first, [(bf16, f32)],
                          tn=1024, n_panels=D_MODEL // 1024)
    h_p, h_s, hn2_p, hn2_s = _resid_norm("mix_residual", mix, (xp, xs), norm_mix_post[0], norm_ffn_pre[0])

    (act,) = _rows_matmul("ffn_gate_up", [(hn2_p, hn2_s)],
                          [(w_gate_up[0], 0, 0), (w_gate_up[0], D_FF // 512, 0)],
                          [(0, 0), (0, 1)], lambda rs, rows, cols: [_silu(rs[0]) * rs[1]], [(bf16, f32)],
                          tn=512, n_panels=D_FF // 512)
    (ffn,) = _rows_matmul("ffn_down", [act], [(w_down[0], 0, 0)], [(0, 0)], first, [(bf16, f32)],
                          tn=512, n_panels=D_MODEL // 512, single_buffer_weights=True)
    out_p, out_s = _resid_norm("ffn_residual", ffn, (h_p, h_s), norm_ffn_post[0], None)

    proj_b = proj_p.reshape(BATCH, SEQ, N_MAIN)
    win_v_p = proj_b[:, SEQ - WINDOW:, OFF_V:OFF_Z].astype(f32)
    conv_p = proj_b[:, SEQ - (CONV_W - 1):, OFF_XBC:OFF_DT].astype(f32)
    conv_s = jnp.concatenate([state_conv[0][:, 1:], proj_s[:, None, OFF_XBC:OFF_DT]], axis=1)
    return (out_p.reshape(BATCH, SEQ, D_MODEL),
            out_s.reshape(DEC_BATCH, 1, D_MODEL),
            win_k_p.reshape(1, BATCH, WINDOW, N_KV_HEADS, HEAD_DIM),
            win_v_p.reshape(1, BATCH, WINDOW, N_KV_HEADS, HEAD_DIM),
            conv_p.reshape(1, BATCH, CONV_W - 1, CONV_DIM),
            ssm_p.reshape(1, BATCH, N_SSM_HEADS, SSM_HEAD_DIM, D_STATE),
            win_k_s.reshape(1, NS, cw, N_KV_HEADS, HEAD_DIM),
            win_v_s.reshape(1, NS, cw, N_KV_HEADS, HEAD_DIM),
            conv_s.reshape(1, NS, CONV_W - 1, CONV_DIM),
            ssm_s.reshape(1, NS, N_SSM_HEADS, SSM_HEAD_DIM, D_STATE))
```

```python
import functools

import numpy as np
import jax
import jax.numpy as jnp
from jax import lax
from jax.experimental import pallas as pl
from jax.experimental.pallas import tpu as pltpu

f32 = jnp.float32
bf16 = jnp.bfloat16

D_MODEL = 2048
BATCH = 4
SEQ = 2048
DEC_BATCH = 128
PAST_LEN = 16384
N_HEADS = 32
N_KV_HEADS = 8
HEAD_DIM = 64
GQA_GROUP = 4
WINDOW = 128
ROPE_THETA = 10000.0
ATTN_WIDTH = N_HEADS * HEAD_DIM
KV_WIDTH = N_KV_HEADS * HEAD_DIM
D_INNER = 4096
SSM_HEAD_DIM = 64
N_SSM_HEADS = 64
D_STATE = 128
N_SSM_GROUPS = 8
HEADS_PER_GROUP = 8
GROUP_WIDTH = HEADS_PER_GROUP * SSM_HEAD_DIM
CONV_W = 4
CONV_DIM = D_INNER + 2 * N_SSM_GROUPS * D_STATE
CHUNK = 128
D_FF = 5632
EPS = 1e-6

OFF_Q = 0
OFF_K = ATTN_WIDTH
OFF_V = OFF_K + KV_WIDTH
OFF_Z = OFF_V + KV_WIDTH
OFF_XBC = OFF_Z + D_INNER
OFF_DT = OFF_XBC + CONV_DIM
OFF_GATES = OFF_DT + N_SSM_HEADS
N_MAIN = OFF_DT

NP = BATCH * SEQ
NS = DEC_BATCH
TP = 1024
NTP = NP // TP
TR = 512
NTR = NP // TR
N_CHUNKS = SEQ // CHUNK

VMEM_LIMIT = 56 * 1024 * 1024


def _cparams(sem):
    return pltpu.CompilerParams(dimension_semantics=sem, vmem_limit_bytes=VMEM_LIMIT)


def _split2(v):
    hi = v.astype(bf16)
    lo = (v - hi.astype(f32)).astype(bf16)
    return hi, lo


def _split3(v):
    hi = v.astype(bf16)
    r = v - hi.astype(f32)
    mid = r.astype(bf16)
    lo = (r - mid.astype(f32)).astype(bf16)
    return hi, mid, lo


def _dot(a, b):
    return jnp.dot(a, b, preferred_element_type=f32)


def _dot_nt(a, b):
    return lax.dot_general(a, b, (((1,), (1,)), ((), ())), preferred_element_type=f32)


def _dot_tn(a, b):
    return lax.dot_general(a, b, (((0,), (0,)), ((), ())), preferred_element_type=f32)


def _dot_f32_lhs(a, b_exact):
    p = _split3(a)
    return _dot(p[0], b_exact) + _dot(p[1], b_exact) + _dot(p[2], b_exact)


def _dot_f32_rhs(a_exact, b):
    p = _split3(b)
    return _dot(a_exact, p[0]) + _dot(a_exact, p[1]) + _dot(a_exact, p[2])


def _sigmoid(x):
    return 0.5 * jnp.tanh(0.5 * x) + 0.5


def _silu(x):
    return x * _sigmoid(x)


def _softplus(x):
    return jnp.maximum(x, 0.0) + jnp.log1p(jnp.exp(-jnp.abs(x)))


def _rms(x, w):
    return x * lax.rsqrt(jnp.mean(x * x, axis=-1, keepdims=True) + EPS) * w


def _head_selector(n_heads, width):
    r = lax.broadcasted_iota(jnp.int32, (n_heads, n_heads * width), 0)
    c = lax.broadcasted_iota(jnp.int32, (n_heads, n_heads * width), 1)
    return jnp.where(c // width == r, 1.0, 0.0).astype(bf16)


def _prenorm_body(xp_ref, xs_ref, nw_ref, wdt_ref, dtb_ref, hnp_ref, hns_ref, dtg_ref, dtt_ref, dtsg_ref):
    i = pl.program_id(0)
    w_hi, w_lo = _split2(wdt_ref[0:N_SSM_HEADS, :])

    def dt_rows(hn, grouped_ref):
        a_hi, a_lo = _split2(hn)
        raw = _dot_nt(a_hi, w_hi) + _dot_nt(a_hi, w_lo) + _dot_nt(a_lo, w_hi)
        dt = _softplus(raw + dtb_ref[...])
        for g in range(N_SSM_GROUPS):
            grouped_ref[g] = dt[:, g * HEADS_PER_GROUP:(g + 1) * HEADS_PER_GROUP]
        return dt

    @pl.when(i < NTP)
    def _():
        hn = _rms(xp_ref[...], nw_ref[...])
        hnp_ref[...] = hn.astype(bf16)
        dtt_ref[...] = dt_rows(hn, dtg_ref).T

    @pl.when(i == NTP)
    def _():
        hn = _rms(xs_ref[...], nw_ref[...])
        hns_ref[...] = hn
        dt_rows(hn, dtsg_ref)


def _prenorm(xp, xs, norm_w, w_in_t, dt_bias):
    pidx = lambda i: (jnp.minimum(i, NTP - 1), 0)
    const = lambda i: (0, 0)
    return pl.pallas_call(
        _prenorm_body,
        grid=(NTP + 1,),
        in_specs=[
            pl.BlockSpec((TP, D_MODEL), pidx),
            pl.BlockSpec((NS, D_MODEL), const),
            pl.BlockSpec((1, D_MODEL), const),
            pl.BlockSpec((128, D_MODEL), lambda i: (OFF_DT // 128, 0)),
            pl.BlockSpec((1, N_SSM_HEADS), const),
        ],
        out_specs=[
            pl.BlockSpec((TP, D_MODEL), pidx),
            pl.BlockSpec((NS, D_MODEL), const),
            pl.BlockSpec((N_SSM_GROUPS, TP, HEADS_PER_GROUP), lambda i: (0, jnp.minimum(i, NTP - 1), 0)),
            pl.BlockSpec((N_SSM_HEADS, TP), lambda i: (0, jnp.minimum(i, NTP - 1))),
            pl.BlockSpec((N_SSM_GROUPS, NS, HEADS_PER_GROUP), lambda i: (0, 0, 0)),
        ],
        out_shape=[
            jax.ShapeDtypeStruct((NP, D_MODEL), bf16),
            jax.ShapeDtypeStruct((NS, D_MODEL), f32),
            jax.ShapeDtypeStruct((N_SSM_GROUPS, NP, HEADS_PER_GROUP), f32),
            jax.ShapeDtypeStruct((N_SSM_HEADS, NP), f32),
            jax.ShapeDtypeStruct((N_SSM_GROUPS, NS, HEADS_PER_GROUP), f32),
        ],
        compiler_params=_cparams(("arbitrary",)),
        name="prenorm_dt",
    )(xp, xs, norm_w.reshape(1, D_MODEL), w_in_t, dt_bias.reshape(1, -1))


CAST_CHUNK = 256


def _rows_matmul_body(*refs, n_a, n_w, n_row, n_col, n_out, dots, epilogue, shifts, transposed):
    pos = 0
    a_p = refs[pos:pos + n_a]; pos += n_a
    a_s = refs[pos:pos + n_a]; pos += n_a
    w = refs[pos:pos + n_w]; pos += n_w
    n_tail = sum(1 for s in shifts if s)
    tails = iter(refs[pos:pos + n_tail]); pos += n_tail
    w_tail = [next(tails) if s else None for s in shifts]
    row_p = refs[pos:pos + n_row]; pos += n_row
    row_s = refs[pos:pos + n_row]; pos += n_row
    col = refs[pos:pos + n_col]; pos += n_col
    out_p = refs[pos:pos + n_out]; pos += n_out
    out_s = refs[pos:pos + n_out]; pos += n_out
    wb = refs[pos:pos + n_w]
    i = pl.program_id(1)

    @pl.when(i == 0)
    def _():
        for k in range(n_w):
            if transposed[k]:
                tn = w[k].shape[0]
                for c in range(tn // CAST_CHUNK):
                    r0 = c * CAST_CHUNK + shifts[k]
                    if r0 + CAST_CHUNK <= tn:
                        blk = w[k][r0:r0 + CAST_CHUNK, :]
                    else:
                        blk = jnp.concatenate([w[k][r0:tn, :], w_tail[k][0:r0 + CAST_CHUNK - tn, :]], axis=0)
                    wb[k][:, c * CAST_CHUNK:(c + 1) * CAST_CHUNK] = blk.T.astype(bf16)
                continue

            def cast(c, carry, k=k):
                r0 = pl.multiple_of(c * CAST_CHUNK, CAST_CHUNK)
                blk = w[k][pl.ds(r0, CAST_CHUNK), :]
                if shifts[k]:
                    blk = jnp.concatenate([blk[:, shifts[k]:], w_tail[k][pl.ds(r0, CAST_CHUNK), 0:shifts[k]]], axis=1)
                wb[k][pl.ds(r0, CAST_CHUNK), :] = blk.astype(bf16)
                return carry

            lax.fori_loop(0, w[k].shape[0] // CAST_CHUNK, cast, 0)

    def compute(a, row, out):
        rs = [_dot(a[ai][...].astype(bf16), wb[wi][...]) for ai, wi in dots]
        res = epilogue(rs, [r[...].astype(f32) for r in row], [c[...] for c in col])
        for o, v in zip(out, res):
            o[...] = v.astype(o.dtype)

    @pl.when(i == 0)
    def _():
        compute(a_s, row_s, out_s)

    @pl.when(i > 0)
    def _():
        compute(a_p, row_p, out_p)


def _rows_matmul(name, a_pairs, weights, dots, epilogue, out_dtypes, tn, n_panels,
                 row_pairs=(), cols=(), tp=TP, single_buffer_weights=False):
    n_a, n_w, n_row, n_col, n_out = len(a_pairs), len(weights), len(row_pairs), len(cols), len(out_dtypes)
    ptile = lambda i: jnp.maximum(i - 1, 0)
    pidx = lambda n, i: (ptile(i), 0)
    sidx = lambda n, i: (0, 0)
    in_specs, args = [], []
    for ap, _ in a_pairs:
        in_specs.append(pl.BlockSpec((tp, ap.shape[1]), pidx)); args.append(ap)
    for _, as_ in a_pairs:
        in_specs.append(pl.BlockSpec((NS, as_.shape[1]), sidx)); args.append(as_)
    w_mode = dict(pipeline_mode=pl.Buffered(1)) if single_buffer_weights else {}
    kdim = lambda wk, tr: wk.shape[1] if tr else wk.shape[0]
    for wk, off, _, tr in weights:
        panel = functools.partial(lambda n, i, off: (0, n + off), off=off)
        spec = (pl.BlockSpec((tn, kdim(wk, tr)), lambda n, i, panel=panel: panel(n, i)[::-1], **w_mode) if tr
                else pl.BlockSpec((kdim(wk, tr), tn), panel, **w_mode))
        in_specs.append(spec); args.append(wk)
    for wk, off, shift, tr in weights:
        if shift:
            after = functools.partial(lambda n, i, off: (0, (n + off + 1) * (tn // 128)), off=off)
            spec = (pl.BlockSpec((128, kdim(wk, tr)), lambda n, i, after=after: after(n, i)[::-1], **w_mode) if tr
                    else pl.BlockSpec((kdim(wk, tr), 128), after, **w_mode))
            in_specs.append(spec); args.append(wk)
    out_p_idx = lambda n, i: (ptile(i), n)
    out_s_idx = lambda n, i: (0, n)
    for rp, _, off in row_pairs:
        in_specs.append(pl.BlockSpec((tp, tn), functools.partial(lambda n, i, off: (ptile(i), n + off), off=off)))
        args.append(rp)
    for _, rs, off in row_pairs:
        in_specs.append(pl.BlockSpec((NS, tn), functools.partial(lambda n, i, off: (0, n + off), off=off)))
        args.append(rs)
    for c in cols:
        in_specs.append(pl.BlockSpec((1, tn), lambda n, i: (0, n))); args.append(c)
    width = n_panels * tn
    out_specs = ([pl.BlockSpec((tp, tn), out_p_idx)] * n_out + [pl.BlockSpec((NS, tn), out_s_idx)] * n_out)
    out_shape = ([jax.ShapeDtypeStruct((NP, width), dp) for dp, _ in out_dtypes]
                 + [jax.ShapeDtypeStruct((NS, width), ds) for _, ds in out_dtypes])
    scratch = [pltpu.VMEM((kdim(wk, tr), tn), bf16) for wk, _, _, tr in weights]
    body = functools.partial(_rows_matmul_body, n_a=n_a, n_w=n_w, n_row=n_row, n_col=n_col, n_out=n_out,
                             dots=tuple(dots), epilogue=epilogue,
                             shifts=tuple(s for _, _, s, _ in weights),
                             transposed=tuple(tr for _, _, _, tr in weights))
    res = pl.pallas_call(
        body,
        grid=(n_panels, NP // tp + 1),
        in_specs=in_specs,
        out_specs=out_specs,
        out_shape=out_shape,
        scratch_shapes=scratch,
        compiler_params=_cparams(("arbitrary", "arbitrary")),
        name=name,
    )(*args)
    return [(res[k], res[n_out + k]) for k in range(n_out)]


def _rope_tables(pos):
    half = HEAD_DIM // 2
    inv = ROPE_THETA ** (-jnp.arange(half, dtype=f32) / half)
    ang = pos.astype(f32)[:, None] * inv[None, :]
    cos, sin = jnp.cos(ang), jnp.sin(ang)
    return jnp.concatenate([cos, cos, cos, cos], axis=1), jnp.concatenate([-sin, sin, -sin, sin], axis=1)


def _rope_chunk(x, cos, sin, first_half):
    from_right = pltpu.roll(x, 96, 1)
    from_left = pltpu.roll(x, 32, 1)
    return x * cos + jnp.where(first_half, from_right, from_left) * sin


def _rope(x, cos, sin):
    rows, width = x.shape
    lane = lax.broadcasted_iota(jnp.int32, (rows, 128), 1)
    first_half = (lane % HEAD_DIM) < (HEAD_DIM // 2)
    return [_rope_chunk(x[:, c * 128:(c + 1) * 128], cos, sin, first_half) for c in range(width // 128)]


def _attn_prompt_body(sink_ref, q_ref, kv_ref, cos_ref, sin_ref, cost_ref, sint_ref, bias_ref,
                      o_ref, wk_ref, kbuf, vbuf_t):
    n = pl.program_id(1)
    cur = n % 2
    prev = 1 - cur

    @pl.when(n == 0)
    def _():
        kbuf[1] = jnp.zeros((WINDOW, KV_WIDTH), bf16)
        vbuf_t[1] = jnp.zeros((KV_WIDTH, WINDOW), bf16)

    kc = _rope(kv_ref[:, 0:KV_WIDTH].astype(f32), cos_ref[...], sin_ref[...])
    for c, v in enumerate(kc):
        kbuf[cur, :, c * 128:(c + 1) * 128] = v.astype(bf16)
    vbuf_t[cur] = kv_ref[:, KV_WIDTH:2 * KV_WIDTH].astype(f32).T.astype(bf16)

    @pl.when(n == N_CHUNKS - 1)
    def _():
        for c, v in enumerate(kc):
            wk_ref[:, c * 128:(c + 1) * 128] = v

    qt = q_ref[...].astype(f32).T
    cost, sint = cost_ref[...], sint_ref[...]
    bias = bias_ref[...]
    scale = HEAD_DIM ** -0.5
    half = HEAD_DIM // 2
    pad = jnp.zeros((HEAD_DIM, CHUNK), f32)
    heads_per_pair = 2 * GQA_GROUP
    heads = []
    for pr in range(N_KV_HEADS // 2):
        qcols = []
        for hh in range(heads_per_pair):
            h = pr * heads_per_pair + hh
            x1 = qt[h * HEAD_DIM:h * HEAD_DIM + half, :]
            x2 = qt[h * HEAD_DIM + half:(h + 1) * HEAD_DIM, :]
            r1 = (x1 * cost - x2 * sint) * scale
            r2 = (x2 * cost + x1 * sint) * scale
            qcols.append(jnp.concatenate([r1, r2, pad] if hh < GQA_GROUP else [pad, r1, r2], axis=0))
        qw = jnp.concatenate(qcols, axis=1).astype(bf16)
        lanes = slice(pr * 128, (pr + 1) * 128)
        kpair = jnp.concatenate([kbuf[prev, :, lanes], kbuf[cur, :, lanes]], axis=0)
        st = _dot(kpair, qw)
        st = st + jnp.concatenate([bias] * heads_per_pair, axis=1)
        sink = sink_ref[:, pr * heads_per_pair * CHUNK:(pr + 1) * heads_per_pair * CHUNK]
        m = jnp.maximum(jnp.max(st, axis=0, keepdims=True), sink)
        p = jnp.exp(st - m)
        inv = 1.0 / (jnp.sum(p, axis=0, keepdims=True) + jnp.exp(sink - m))
        pb = p.astype(bf16)
        for k in range(2):
            g = 2 * pr + k
            cols = slice(k * GQA_GROUP * CHUNK, (k + 1) * GQA_GROUP * CHUNK)
            dims = slice(g * HEAD_DIM, (g + 1) * HEAD_DIM)
            vgt = jnp.concatenate([vbuf_t[prev, dims, :], vbuf_t[cur, dims, :]], axis=1)
            og = _dot(vgt, pb[:, cols]) * inv[:, cols]
            heads += [og[:, j * CHUNK:(j + 1) * CHUNK] for j in range(GQA_GROUP)]
    o_ref[...] = jnp.concatenate(heads, axis=0).T.astype(o_ref.dtype)


def _band_bias():
    qi = np.arange(WINDOW)[None, :]
    kj = np.arange(2 * WINDOW)[:, None]
    cur = (kj >= WINDOW) & (kj - WINDOW <= qi)
    prev = (kj < WINDOW) & (kj > qi)
    neg = np.float32(-np.inf)
    b0 = np.where(cur, np.float32(0), neg)
    b1 = np.where(cur | prev, np.float32(0), neg)
    return jnp.asarray(np.stack([b0, b1]).astype(np.float32))


def _attn_prompt(proj_p, sinks):
    pos = jnp.arange(SEQ, dtype=jnp.int32)
    cos, sin = _rope_tables(pos)
    half = HEAD_DIM // 2
    cos_t, sin_t = cos[:, :half].T, sin[:, half:HEAD_DIM].T
    row = lambda b, n: b * N_CHUNKS + n
    return pl.pallas_call(
        _attn_prompt_body,
        grid=(BATCH, N_CHUNKS),
        in_specs=[
            pl.BlockSpec((1, N_HEADS * CHUNK), lambda b, n: (0, 0)),
            pl.BlockSpec((CHUNK, ATTN_WIDTH), lambda b, n: (row(b, n), 0)),
            pl.BlockSpec((CHUNK, 2 * KV_WIDTH), lambda b, n: (row(b, n), OFF_K // (2 * KV_WIDTH))),
            pl.BlockSpec((CHUNK, 128), lambda b, n: (n, 0)),
            pl.BlockSpec((CHUNK, 128), lambda b, n: (n, 0)),
            pl.BlockSpec((half, CHUNK), lambda b, n: (0, n)),
            pl.BlockSpec((half, CHUNK), lambda b, n: (0, n)),
            pl.BlockSpec((None, 2 * WINDOW, CHUNK), lambda b, n: (jnp.minimum(n, 1), 0, 0)),
        ],
        out_specs=[
            pl.BlockSpec((CHUNK, ATTN_WIDTH), lambda b, n: (row(b, n), 0)),
            pl.BlockSpec((None, WINDOW, KV_WIDTH), lambda b, n: (b, 0, 0)),
        ],
        out_shape=[
            jax.ShapeDtypeStruct((NP, ATTN_WIDTH), bf16),
            jax.ShapeDtypeStruct((BATCH, WINDOW, KV_WIDTH), f32),
        ],
        scratch_shapes=[pltpu.VMEM((2, WINDOW, KV_WIDTH), bf16), pltpu.VMEM((2, KV_WIDTH, WINDOW), bf16)],
        compiler_params=_cparams(("arbitrary", "arbitrary")),
        name="attn_prompt",
    )(jnp.repeat(sinks, CHUNK).reshape(1, N_HEADS * CHUNK), proj_p, proj_p, cos, sin, cos_t, sin_t, _band_bias())


SB = 8


def _dup_halves(x, lane_first):
    swapped = pltpu.roll(x, 64, x.ndim - 1)
    return jnp.where(lane_first, x, swapped), jnp.where(lane_first, swapped, x)


def _attn_sample_body(q_ref, k_ref, v_ref, ck_ref, cv_ref, cos_ref, sin_ref, sink_ref, e_ref, et_ref,
                      o_ref, wk_ref, wv_ref):
    cos, sin = cos_ref[...], sin_ref[...]
    scale = HEAD_DIM ** -0.5
    qc = [c * scale for c in _rope(q_ref[...], cos, sin)]
    kc = _rope(k_ref[...], cos, sin)
    vn = v_ref[...]
    cw = ck_ref.shape[1]
    lane3 = lax.broadcasted_iota(jnp.int32, (SB, cw, 128), 2) < HEAD_DIM
    lane2 = lax.broadcasted_iota(jnp.int32, (SB, 128), 1) < HEAD_DIM

    s = jnp.zeros((SB * cw, 128), f32)
    sn = jnp.zeros((SB, 128), f32)
    for v in range(KV_WIDTH // 128):
        kd = _dup_halves(ck_ref[:, :, v * 128:(v + 1) * 128], lane3)
        knd = _dup_halves(kc[v], lane2)
        for half in range(2):
            for t in range(2):
                c = 4 * v + 2 * half + t
                e_c = e_ref[c * 128:(c + 1) * 128, :]
                prod = (kd[half] * qc[c][:, None, :]).astype(bf16).reshape(SB * cw, 128)
                s = s + _dot(prod, e_c)
                sn = sn + _dot((knd[half] * qc[c]).astype(bf16), e_c)
    s = s.reshape(SB, cw, 128)
    key = lax.broadcasted_iota(jnp.int32, (SB, cw, 128), 1)
    s = jnp.where(key == 0, -jnp.inf, s)
    sink = sink_ref[...]
    m = jnp.maximum(jnp.maximum(jnp.max(s, axis=1), sn), sink)
    p = jnp.exp(s - m[:, None, :])
    pn = jnp.exp(sn - m)
    den = jnp.sum(p, axis=1) + pn + jnp.exp(sink - m)
    inv = 1.0 / den
    p = (p * inv[:, None, :]).astype(bf16).reshape(SB * cw, 128)
    pn = (pn * inv).astype(bf16)

    for v in range(KV_WIDTH // 128):
        vd = _dup_halves(cv_ref[:, :, v * 128:(v + 1) * 128], lane3)
        vnd = _dup_halves(vn[:, v * 128:(v + 1) * 128], lane2)
        for half in range(2):
            for t in range(2):
                c = 4 * v + 2 * half + t
                et_c = et_ref[:, c * 128:(c + 1) * 128]
                pe = _dot(p, et_c).reshape(SB, cw, 128)
                pne = _dot(pn, et_c)
                o_ref[:, c * 128:(c + 1) * 128] = jnp.sum(pe * vd[half], axis=1) + pne * vnd[half]

    wk_ref[:, 0:cw - 1, :] = ck_ref[:, 1:cw, :]
    wv_ref[:, 0:cw - 1, :] = cv_ref[:, 1:cw, :]
    for c, val in enumerate(kc):
        wk_ref[:, cw - 1:cw, c * 128:(c + 1) * 128] = val[:, None, :]
    wv_ref[:, cw - 1:cw, :] = vn[:, None, :]


def _attn_sample(proj_s, cache_k, cache_v, cos, sin, sinks):
    cw = cache_k.shape[1]
    sel = np.zeros((ATTN_WIDTH, 128), np.float32)
    sel[np.arange(ATTN_WIDTH), np.arange(ATTN_WIDTH) // HEAD_DIM] = 1.0
    e = jnp.asarray(sel, dtype=bf16)
    et = jnp.asarray(sel.T, dtype=bf16)
    sink_row = jnp.zeros((1, 128), f32).at[0, :N_HEADS].set(sinks)
    rows = lambda i: (i, 0)
    rows3 = lambda i: (i, 0, 0)
    const = lambda i: (0, 0)
    return pl.pallas_call(
        _attn_sample_body,
        grid=(NS // SB,),
        in_specs=[
            pl.BlockSpec((SB, ATTN_WIDTH), rows),
            pl.BlockSpec((SB, KV_WIDTH), lambda i: (i, OFF_K // KV_WIDTH)),
            pl.BlockSpec((SB, KV_WIDTH), lambda i: (i, OFF_V // KV_WIDTH)),
            pl.BlockSpec((SB, cw, KV_WIDTH), rows3),
            pl.BlockSpec((SB, cw, KV_WIDTH), rows3),
            pl.BlockSpec((1, 128), const),
            pl.BlockSpec((1, 128), const),
            pl.BlockSpec((1, 128), const),
            pl.BlockSpec((ATTN_WIDTH, 128), const),
            pl.BlockSpec((128, ATTN_WIDTH), const),
        ],
        out_specs=[
            pl.BlockSpec((SB, ATTN_WIDTH), rows),
            pl.BlockSpec((SB, cw, KV_WIDTH), rows3),
            pl.BlockSpec((SB, cw, KV_WIDTH), rows3),
        ],
        out_shape=[
            jax.ShapeDtypeStruct((NS, ATTN_WIDTH), f32),
            jax.ShapeDtypeStruct((NS, cw, KV_WIDTH), f32),
            jax.ShapeDtypeStruct((NS, cw, KV_WIDTH), f32),
        ],
        compiler_params=_cparams(("arbitrary",)),
        name="attn_sample",
    )(proj_s, proj_s, proj_s, cache_k, cache_v, cos, sin, sink_row, e, et)


GPS = 2
XW = GPS * GROUP_WIDTH
BW = GPS * D_STATE
XBCW = XW + 2 * BW
CPS = 4
TAIL = 16


def _ssd_prompt_body(x_ref, b_ref, c_ref, z_ref, dt_ref, dtt_ref, cwx_ref, cwb_ref, cwc_ref,
                     cbx_ref, cbb_ref, cbc_ref, alogt_ref, dsk_ref, nw_ref,
                     y_ref, st_ref, hst, tails):
    c = pl.program_id(2)
    cur = c % 2
    prev = 1 - cur

    @pl.when(c == 0)
    def _():
        hst[1] = jnp.zeros((GPS, D_STATE, GROUP_WIDTH), f32)
        tails[1] = jnp.zeros((TAIL, XBCW), bf16)

    ti = lax.broadcasted_iota(jnp.int32, (CONV_W * CHUNK, 2 * CHUNK), 0)
    si = lax.broadcasted_iota(jnp.int32, (CONV_W * CHUNK, 2 * CHUNK), 1)
    pick = jnp.where(si == CHUNK + (ti % CHUNK) - (ti // CHUNK), 1.0, 0.0).astype(bf16)
    w = jnp.concatenate([cwx_ref[...], cwb_ref[...], cwc_ref[...]], axis=1)
    bias = jnp.concatenate([cbx_ref[...], cbb_ref[...], cbc_ref[...]], axis=1)
    li = lax.broadcasted_iota(jnp.int32, (CHUNK, CHUNK), 0)
    si = lax.broadcasted_iota(jnp.int32, (CHUNK, CHUNK), 1)
    causal = li >= si
    trit = jnp.where(li <= si, 1.0, 0.0).astype(bf16)
    sel = _head_selector(HEADS_PER_GROUP, SSM_HEAD_DIM)

    def expand(v):
        hi = v.astype(bf16)
        lo = (v - hi.astype(f32)).astype(bf16)
        return _dot(hi, sel) + _dot(lo, sel)

    expand_row = lambda v: _dot_f32_lhs(jnp.broadcast_to(v, (8, HEADS_PER_GROUP)), sel)[0:1, :]
    lane = lax.broadcasted_iota(jnp.int32, (CHUNK, 128), 1)
    zero = jnp.zeros((CHUNK, 128), bf16)
    log2e = 1.4426950408889634

    tail = tails[prev]
    h = [hst[prev, k] for k in range(GPS)]
    for cc in range(CPS):
        rows = slice(cc * CHUNK, (cc + 1) * CHUNK)
        xbc = jnp.concatenate([x_ref[rows, :], b_ref[rows, :], c_ref[rows, :]], axis=1)
        stacked = jnp.concatenate([jnp.zeros((CHUNK - TAIL, XBCW), bf16), tail, xbc], axis=0)
        tail = xbc[CHUNK - TAIL:, :]
        taps = _dot(pick, stacked)
        acc = bias
        for j in range(CONV_W):
            acc = acc + w[CONV_W - 1 - j:CONV_W - j, :] * taps[j * CHUNK:(j + 1) * CHUNK, :]
        act = _silu(acc)

        for k in range(GPS):
            xs = act[:, k * GROUP_WIDTH:(k + 1) * GROUP_WIDTH]
            bmb = act[:, XW + k * D_STATE:XW + (k + 1) * D_STATE].astype(bf16)
            cmb = act[:, XW + BW + k * D_STATE:XW + BW + (k + 1) * D_STATE].astype(bf16)
            dtt = dtt_ref[k * HEADS_PER_GROUP:(k + 1) * HEADS_PER_GROUP, rows]
            dat = dtt * (-jnp.exp(alogt_ref[k]))
            acumt = _dot_f32_lhs(dat, trit)
            acum = acumt.T
            a_last = acum[CHUNK - 1:CHUNK, :]
            acum2, acumt2 = acum * log2e, acumt * log2e

            xb = xs.astype(bf16)
            xdec = (xs * expand(dt_ref[k, rows, :] * jnp.exp(a_last - acum))).astype(bf16)

            cb = _dot_nt(cmb, bmb)
            pairs = []
            for pr in range(HEADS_PER_GROUP // 2):
                ms = []
                for r in (2 * pr, 2 * pr + 1):
                    seg = acum2[:, r:r + 1] - acumt2[r:r + 1, :]
                    decay = jnp.exp2(jnp.where(causal, seg, -jnp.inf))
                    ms.append((cb * decay * dtt[r:r + 1, :]).astype(bf16))
                xp = xb[:, pr * 128:(pr + 1) * 128]
                x0 = jnp.where(lane < SSM_HEAD_DIM, xp, zero)
                x1 = jnp.where(lane < SSM_HEAD_DIM, zero, xp)
                pairs.append(_dot(jnp.concatenate(ms, axis=1), jnp.concatenate([x0, x1], axis=0)))
            y = jnp.concatenate(pairs, axis=1)

            h_prev = h[k]
            y = y + _dot(cmb, h_prev.astype(bf16)) * expand(jnp.exp2(acum2))
            y = y + expand_row(dsk_ref[k]) * xs

            cols = slice(k * GROUP_WIDTH, (k + 1) * GROUP_WIDTH)
            gz = y * _silu(z_ref[rows, cols].astype(f32))
            y_ref[rows, cols] = _rms(gz, nw_ref[:, cols]).astype(y_ref.dtype)

            h_new = h_prev * expand_row(jnp.exp(a_last)) + _dot_tn(bmb, xdec)
            h[k] = h_new

    tails[cur] = tail
    for k in range(GPS):
        hst[cur, k] = h[k]

    @pl.when(c == N_CHUNKS // CPS - 1)
    def _():
        for k in range(GPS):
            st_ref[k] = hst[cur, k].T


def _ssd_prompt(proj_p, dtg, dtt, conv_w, conv_b, a_log, d_skip, ssm_norm):
    steps = N_CHUNKS // CPS
    rows = CPS * CHUNK
    row = lambda b, g, c: b * steps + c
    xoff = OFF_XBC // XW
    boff = (OFF_XBC + D_INNER) // BW
    coff = boff + N_SSM_GROUPS // GPS
    zoff = OFF_Z // XW
    wboff = D_INNER // BW
    wcoff = wboff + N_SSM_GROUPS // GPS
    alog_t = a_log.reshape(N_SSM_GROUPS, HEADS_PER_GROUP, 1)
    dsk_g = d_skip.reshape(N_SSM_GROUPS, 1, HEADS_PER_GROUP)
    conv_b = conv_b.reshape(1, CONV_DIM)
    grp = lambda b, g, c: (g, 0, 0)
    return pl.pallas_call(
        _ssd_prompt_body,
        grid=(BATCH, N_SSM_GROUPS // GPS, steps),
        in_specs=[
            pl.BlockSpec((rows, XW), lambda b, g, c: (row(b, g, c), xoff + g)),
            pl.BlockSpec((rows, BW), lambda b, g, c: (row(b, g, c), boff + g)),
            pl.BlockSpec((rows, BW), lambda b, g, c: (row(b, g, c), coff + g)),
            pl.BlockSpec((rows, XW), lambda b, g, c: (row(b, g, c), zoff + g)),
            pl.BlockSpec((GPS, rows, HEADS_PER_GROUP), lambda b, g, c: (g, row(b, g, c), 0)),
            pl.BlockSpec((GPS * HEADS_PER_GROUP, rows), lambda b, g, c: (g, row(b, g, c))),
            pl.BlockSpec((CONV_W, XW), lambda b, g, c: (0, g)),
            pl.BlockSpec((CONV_W, BW), lambda b, g, c: (0, wboff + g)),
            pl.BlockSpec((CONV_W, BW), lambda b, g, c: (0, wcoff + g)),
            pl.BlockSpec((1, XW), lambda b, g, c: (0, g)),
            pl.BlockSpec((1, BW), lambda b, g, c: (0, wboff + g)),
            pl.BlockSpec((1, BW), lambda b, g, c: (0, wcoff + g)),
            pl.BlockSpec((GPS, HEADS_PER_GROUP, 1), grp),
            pl.BlockSpec((GPS, 1, HEADS_PER_GROUP), grp),
            pl.BlockSpec((1, XW), lambda b, g, c: (0, g)),
        ],
        out_specs=[
            pl.BlockSpec((rows, XW), lambda b, g, c: (row(b, g, c), g)),
            pl.BlockSpec((None, GPS, GROUP_WIDTH, D_STATE), lambda b, g, c: (b, g, 0, 0)),
        ],
        out_shape=[
            jax.ShapeDtypeStruct((NP, D_INNER), bf16),
            jax.ShapeDtypeStruct((BATCH, N_SSM_GROUPS, GROUP_WIDTH, D_STATE), f32),
        ],
        scratch_shapes=[
            pltpu.VMEM((2, GPS, D_STATE, GROUP_WIDTH), f32),
            pltpu.VMEM((2, TAIL, XBCW), bf16),
        ],
        compiler_params=_cparams(("arbitrary", "arbitrary", "arbitrary")),
        name="ssd_prompt",
    )(proj_p, proj_p, proj_p, proj_p, dtg, dtt, conv_w, conv_w, conv_w, conv_b, conv_b, conv_b,
      alog_t, dsk_g, ssm_norm.reshape(1, D_INNER))


SQ = 16
GSS = 2


def _ssd_sample_body(x_ref, b_ref, c_ref, sx_ref, sb_ref, sc_ref, z_ref, dt_ref, h_ref,
                     cwx_ref, cwb_ref, cwc_ref, cbx_ref, cbb_ref, cbc_ref, alog_ref, dsk_ref, nw_ref,
                     y_ref, ho_ref):
    def conv(new_ref, st_ref, w_ref, bias_ref):
        w = w_ref[...]
        acc = bias_ref[...] + w[CONV_W - 1:CONV_W, :] * new_ref[...]
        for j in range(CONV_W - 1):
            acc = acc + w[j:j + 1, :] * st_ref[j]
        return _silu(acc)

    xs_all = conv(x_ref, sx_ref, cwx_ref, cbx_ref)
    bm_all = conv(b_ref, sb_ref, cwb_ref, cbb_ref)
    cm_all = conv(c_ref, sc_ref, cwc_ref, cbc_ref)
    zs_all = _silu(z_ref[...])
    sel = _head_selector(HEADS_PER_GROUP, SSM_HEAD_DIM)
    exact = lambda v: v.astype(bf16).astype(f32)
    zeros = jnp.zeros((SQ, D_STATE), f32)
    ones = jnp.ones((SQ, D_STATE), f32)
    wide = (6 * SQ, SQ * 2 * D_STATE)
    row_seq = lax.broadcasted_iota(jnp.int32, wide, 0) % SQ
    col_seq = lax.broadcasted_iota(jnp.int32, wide, 1) // (2 * D_STATE)
    c_row = lax.broadcasted_iota(jnp.int32, (SQ, SQ * D_STATE), 0)
    c_col = lax.broadcasted_iota(jnp.int32, (SQ, SQ * D_STATE), 1) // D_STATE

    for k in range(GSS):
        xs = xs_all[:, k * GROUP_WIDTH:(k + 1) * GROUP_WIDTH]
        bm = bm_all[:, k * D_STATE:(k + 1) * D_STATE]
        cm = cm_all[:, k * D_STATE:(k + 1) * D_STATE]
        dt = dt_ref[k]
        decay = jnp.exp(dt * (-jnp.exp(alog_ref[k])))
        xdt = xs * _dot_f32_lhs(dt, sel)
        dec = _dot_f32_lhs(decay, sel)

        x_hi = exact(xdt)
        d_hi = exact(dec)
        d_mid = exact(dec - d_hi)
        b_hi = exact(bm)
        lhs = jnp.concatenate([x_hi, x_hi, xdt - x_hi, d_hi, d_mid, dec - d_hi - d_mid], axis=0)
        lhs_t = lhs.T.astype(bf16)
        outer_rows = [jnp.concatenate([v, zeros], axis=1) for v in (b_hi, bm - b_hi, b_hi)]
        rhs = jnp.concatenate(outer_rows + [jnp.concatenate([zeros, ones], axis=1)] * 3, axis=0)
        rhs_wide = jnp.where(row_seq == col_seq, jnp.concatenate([rhs] * SQ, axis=1), 0.0).astype(bf16)
        both = _dot(lhs_t, rhs_wide)
        h_bf = []
        rows = slice(k * GROUP_WIDTH, (k + 1) * GROUP_WIDTH)
        for s in range(SQ):
            upd = both[:, s * 2 * D_STATE:(s + 1) * 2 * D_STATE]
            h_new = h_ref[s, rows, :] * upd[:, D_STATE:] + upd[:, 0:D_STATE]
            ho_ref[s, rows, :] = h_new
            h_bf.append(h_new.astype(bf16))
        c_diag = jnp.where(c_row == c_col, jnp.concatenate([cm] * SQ, axis=1), 0.0).astype(bf16)
        y = _dot_nt(c_diag, jnp.concatenate(h_bf, axis=1))
        dsk = _dot_f32_lhs(jnp.broadcast_to(dsk_ref[k], (SQ, HEADS_PER_GROUP)), sel)
        gz = (y + dsk * xs) * zs_all[:, rows]
        y_ref[:, rows] = _rms(gz, nw_ref[:, rows])


def _ssd_sample(proj_s, dtsg, state_conv, state_ssm, conv_w, conv_b, a_log, d_skip, ssm_norm):
    xw, bw = GSS * GROUP_WIDTH, GSS * D_STATE
    xoff = OFF_XBC // xw
    boff = (OFF_XBC + D_INNER) // bw
    coff = boff + N_SSM_GROUPS // GSS
    zoff = OFF_Z // xw
    wboff = D_INNER // bw
    wcoff = wboff + N_SSM_GROUPS // GSS
    conv_b = conv_b.reshape(1, CONV_DIM)
    grp = lambda i, g: (g, 0, 0)
    return pl.pallas_call(
        _ssd_sample_body,
        grid=(NS // SQ, N_SSM_GROUPS // GSS),
        in_specs=[
            pl.BlockSpec((SQ, xw), lambda i, g: (i, xoff + g)),
            pl.BlockSpec((SQ, bw), lambda i, g: (i, boff + g)),
            pl.BlockSpec((SQ, bw), lambda i, g: (i, coff + g)),
            pl.BlockSpec((CONV_W - 1, SQ, xw), lambda i, g: (0, i, g)),
            pl.BlockSpec((CONV_W - 1, SQ, bw), lambda i, g: (0, i, wboff + g)),
            pl.BlockSpec((CONV_W - 1, SQ, bw), lambda i, g: (0, i, wcoff + g)),
            pl.BlockSpec((SQ, xw), lambda i, g: (i, zoff + g)),
            pl.BlockSpec((GSS, SQ, HEADS_PER_GROUP), lambda i, g: (g, i, 0)),
            pl.BlockSpec((SQ, xw, D_STATE), lambda i, g: (i, g, 0)),
            pl.BlockSpec((CONV_W, xw), lambda i, g: (0, g)),
            pl.BlockSpec((CONV_W, bw), lambda i, g: (0, wboff + g)),
            pl.BlockSpec((CONV_W, bw), lambda i, g: (0, wcoff + g)),
            pl.BlockSpec((1, xw), lambda i, g: (0, g)),
            pl.BlockSpec((1, bw), lambda i, g: (0, wboff + g)),
            pl.BlockSpec((1, bw), lambda i, g: (0, wcoff + g)),
            pl.BlockSpec((GSS, 1, HEADS_PER_GROUP), grp),
            pl.BlockSpec((GSS, 1, HEADS_PER_GROUP), grp),
            pl.BlockSpec((1, xw), lambda i, g: (0, g)),
        ],
        out_specs=[
            pl.BlockSpec((SQ, xw), lambda i, g: (i, g)),
            pl.BlockSpec((SQ, xw, D_STATE), lambda i, g: (i, g, 0)),
        ],
        out_shape=[
            jax.ShapeDtypeStruct((NS, D_INNER), f32),
            jax.ShapeDtypeStruct((NS, D_INNER, D_STATE), f32),
        ],
        compiler_params=_cparams(("arbitrary", "arbitrary")),
        name="ssd_sample",
    )(proj_s, proj_s, proj_s, state_conv, state_conv, state_conv, proj_s, dtsg,
      state_ssm.reshape(NS, D_INNER, D_STATE), conv_w, conv_w, conv_w, conv_b, conv_b, conv_b,
      a_log.reshape(N_SSM_GROUPS, 1, HEADS_PER_GROUP), d_skip.reshape(N_SSM_GROUPS, 1, HEADS_PER_GROUP),
      ssm_norm.reshape(1, D_INNER))


def _resid_norm_body(mp, ms, xp, xs, w, op, os_):
    i = pl.program_id(0)

    @pl.when(i < NTR)
    def _():
        op[...] = xp[...] + _rms(mp[...].astype(f32), w[...])

    @pl.when(i == NTR)
    def _():
        os_[...] = xs[...] + _rms(ms[...].astype(f32), w[...])


def _resid_norm(name, m_pair, x_pair, w_post):
    pidx = lambda i: (jnp.minimum(i, NTR - 1), 0)
    const = lambda i: (0, 0)
    return pl.pallas_call(
        _resid_norm_body,
        grid=(NTR + 1,),
        in_specs=[
            pl.BlockSpec((TR, D_MODEL), pidx), pl.BlockSpec((NS, D_MODEL), const),
            pl.BlockSpec((TR, D_MODEL), pidx), pl.BlockSpec((NS, D_MODEL), const),
            pl.BlockSpec((1, D_MODEL), const),
        ],
        out_specs=[pl.BlockSpec((TR, D_MODEL), pidx), pl.BlockSpec((NS, D_MODEL), const)],
        out_shape=[jax.ShapeDtypeStruct((NP, D_MODEL), f32), jax.ShapeDtypeStruct((NS, D_MODEL), f32)],
        compiler_params=_cparams(("arbitrary",)),
        name=name,
    )(m_pair[0], m_pair[1], x_pair[0], x_pair[1], w_post.reshape(1, D_MODEL))


def kernel(x_prompt, x_sample, cache_win_k, cache_win_v, state_conv, state_ssm, norm_mix_pre, norm_mix_post,
           w_in, attn_sinks, w_attn_branch, conv_w, conv_b, dt_bias, a_log, d_skip, ssm_norm, w_ssm_branch,
           w_out, norm_ffn_pre, norm_ffn_post, w_gate_up, w_down):
    assert x_prompt.shape == (BATCH, SEQ, D_MODEL) and x_sample.shape == (DEC_BATCH, 1, D_MODEL)
    assert w_in.shape[0] == 1, "one trunk layer"
    cw = cache_win_k.shape[2]
    xp = x_prompt.reshape(NP, D_MODEL)
    xs = x_sample.reshape(NS, D_MODEL)
    w_in_t = jnp.swapaxes(w_in[0], 0, 1)
    sct = jnp.swapaxes(state_conv[0], 0, 1)

    hn_p, hn_s, dtg, dtt, dtsg = _prenorm(xp, xs, norm_mix_pre[0], w_in_t, dt_bias[0])

    first = lambda rs, rows, cols: [rs[0]]
    (proj,) = _rows_matmul("in_proj", [(hn_p, hn_s)], [(w_in_t, 0, 0, True)], [(0, 0)], first, [(bf16, f32)],
                           tn=1024, n_panels=N_MAIN // 1024)
    proj_p, proj_s = proj

    attn_p, win_k_p = _attn_prompt(proj_p, attn_sinks[0])
    cos_s, sin_s = _rope_tables(jnp.full((1,), PAST_LEN, dtype=jnp.int32))
    attn_s, win_k_s, win_v_s = _attn_sample(
        proj_s, cache_win_k[0].reshape(NS, cw, KV_WIDTH), cache_win_v[0].reshape(NS, cw, KV_WIDTH),
        cos_s, sin_s, attn_sinks[0])

    y_p, ssm_p = _ssd_prompt(proj_p, dtg, dtt, conv_w[0], conv_b[0], a_log[0], d_skip[0], ssm_norm[0])
    y_s, ssm_s = _ssd_sample(proj_s, dtsg, sct, state_ssm[0], conv_w[0], conv_b[0], a_log[0],
                             d_skip[0], ssm_norm[0])

    (gates,) = _rows_matmul("gates", [(hn_p, hn_s)], [(w_in_t, OFF_DT // 1024, OFF_GATES - OFF_DT, True)], [(0, 0)],
                            lambda rs, rows, cols: [_sigmoid(rs[0])], [(bf16, f32)],
                            tn=1024, n_panels=2 * D_MODEL // 1024)
    (ssm_d,) = _rows_matmul("ssm_branch", [(y_p, y_s)], [(w_ssm_branch[0], 0, 0, False)], [(0, 0)],
                            lambda rs, rows, cols: [rows[0] * rs[0]], [(bf16, f32)],
                            tn=512, n_panels=D_MODEL // 512, row_pairs=[gates + (D_MODEL // 512,)])
    (merged,) = _rows_matmul("attn_branch_merge", [(attn_p, attn_s)], [(w_attn_branch[0], 0, 0, False)], [(0, 0)],
                             lambda rs, rows, cols: [rows[0] * rs[0] + rows[1]], [(bf16, f32)],
                             tn=1024, n_panels=D_MODEL // 1024, row_pairs=[gates + (0,), ssm_d + (0,)])
    def mix_residual(rs, rows, cols):
        h = rows[0] + _rms(rs[0], cols[0])
        return [h, _rms(h, cols[1])]

    (h_p, h_s), (hn2_p, hn2_s) = _rows_matmul(
        "out_proj_residual", [merged], [(w_out[0], 0, 0, False)], [(0, 0)], mix_residual, [(f32, f32), (bf16, f32)],
        tn=D_MODEL, n_panels=1, row_pairs=[(xp, xs, 0)],
        cols=[norm_mix_post[0].reshape(1, D_MODEL), norm_ffn_pre[0].reshape(1, D_MODEL)],
        tp=256, single_buffer_weights=True)

    (act,) = _rows_matmul("ffn_gate_up", [(hn2_p, hn2_s)],
                          [(w_gate_up[0], 0, 0, False), (w_gate_up[0], D_FF // 512, 0, False)],
                          [(0, 0), (0, 1)], lambda rs, rows, cols: [_silu(rs[0]) * rs[1]], [(bf16, f32)],
                          tn=512, n_panels=D_FF // 512)
    (ffn,) = _rows_matmul("ffn_down", [act], [(w_down[0], 0, 0, False)], [(0, 0)], first, [(bf16, f32)],
                          tn=512, n_panels=D_MODEL // 512, single_buffer_weights=True)
    out_p, out_s = _resid_norm("ffn_residual", ffn, (h_p, h_s), norm_ffn_post[0])

    proj_b = proj_p.reshape(BATCH, SEQ, N_MAIN)
    win_v_p = proj_b[:, SEQ - WINDOW:, OFF_V:OFF_Z].astype(f32)
    conv_p = proj_b[:, SEQ - (CONV_W - 1):, OFF_XBC:OFF_DT].astype(f32)
    conv_s = jnp.swapaxes(jnp.concatenate([sct[1:], proj_s[None, :, OFF_XBC:OFF_DT]], axis=0), 0, 1)
    return (out_p.reshape(BATCH, SEQ, D_MODEL),
            out_s.reshape(DEC_BATCH, 1, D_MODEL),
            win_k_p.reshape(1, BATCH, WINDOW, N_KV_HEADS, HEAD_DIM),
            win_v_p.reshape(1, BATCH, WINDOW, N_KV_HEADS, HEAD_DIM),
            conv_p.reshape(1, BATCH, CONV_W - 1, CONV_DIM),
            ssm_p.reshape(1, BATCH, N_SSM_HEADS, SSM_HEAD_DIM, D_STATE),
            win_k_s.reshape(1, NS, cw, N_KV_HEADS, HEAD_DIM),
            win_v_s.reshape(1, NS, cw, N_KV_HEADS, HEAD_DIM),
            conv_s.reshape(1, NS, CONV_W - 1, CONV_DIM),
            ssm_s.reshape(1, NS, N_SSM_HEADS, SSM_HEAD_DIM, D_STATE))
```

```python
import functools

import numpy as np
import jax
import jax.numpy as jnp
from jax import lax
from jax.experimental import pallas as pl
from jax.experimental.pallas import tpu as pltpu

f32 = jnp.float32
bf16 = jnp.bfloat16

D_MODEL = 2048
BATCH = 4
SEQ = 2048
DEC_BATCH = 128
PAST_LEN = 16384
N_HEADS = 32
N_KV_HEADS = 8
HEAD_DIM = 64
GQA_GROUP = 4
WINDOW = 128
ROPE_THETA = 10000.0
ATTN_WIDTH = N_HEADS * HEAD_DIM
KV_WIDTH = N_KV_HEADS * HEAD_DIM
D_INNER = 4096
SSM_HEAD_DIM = 64
N_SSM_HEADS = 64
D_STATE = 128
N_SSM_GROUPS = 8
HEADS_PER_GROUP = 8
GROUP_WIDTH = HEADS_PER_GROUP * SSM_HEAD_DIM
CONV_W = 4
CONV_DIM = D_INNER + 2 * N_SSM_GROUPS * D_STATE
CHUNK = 128
D_FF = 5632
EPS = 1e-6

OFF_Q = 0
OFF_K = ATTN_WIDTH
OFF_V = OFF_K + KV_WIDTH
OFF_Z = OFF_V + KV_WIDTH
OFF_XBC = OFF_Z + D_INNER
OFF_DT = OFF_XBC + CONV_DIM
OFF_GATES = OFF_DT + N_SSM_HEADS
N_MAIN = OFF_DT

NP = BATCH * SEQ
NS = DEC_BATCH
TP = 1024
NTP = NP // TP
TR = 512
NTR = NP // TR
N_CHUNKS = SEQ // CHUNK

VMEM_LIMIT = 56 * 1024 * 1024


def _cparams(sem):
    return pltpu.CompilerParams(dimension_semantics=sem, vmem_limit_bytes=VMEM_LIMIT)


def _split2(v):
    hi = v.astype(bf16)
    lo = (v - hi.astype(f32)).astype(bf16)
    return hi, lo


def _split3(v):
    hi = v.astype(bf16)
    r = v - hi.astype(f32)
    mid = r.astype(bf16)
    lo = (r - mid.astype(f32)).astype(bf16)
    return hi, mid, lo


def _dot(a, b):
    return jnp.dot(a, b, preferred_element_type=f32)


def _dot_nt(a, b):
    return lax.dot_general(a, b, (((1,), (1,)), ((), ())), preferred_element_type=f32)


def _dot_tn(a, b):
    return lax.dot_general(a, b, (((0,), (0,)), ((), ())), preferred_element_type=f32)


def _dot_f32_lhs(a, b_exact):
    p = _split3(a)
    return _dot(p[0], b_exact) + _dot(p[1], b_exact) + _dot(p[2], b_exact)


def _dot_f32_rhs(a_exact, b):
    p = _split3(b)
    return _dot(a_exact, p[0]) + _dot(a_exact, p[1]) + _dot(a_exact, p[2])


def _sigmoid(x):
    return 0.5 * jnp.tanh(0.5 * x) + 0.5


def _silu(x):
    return x * _sigmoid(x)


def _softplus(x):
    return jnp.maximum(x, 0.0) + jnp.log1p(jnp.exp(-jnp.abs(x)))


def _rms(x, w):
    return x * lax.rsqrt(jnp.mean(x * x, axis=-1, keepdims=True) + EPS) * w


def _head_selector(n_heads, width):
    r = lax.broadcasted_iota(jnp.int32, (n_heads, n_heads * width), 0)
    c = lax.broadcasted_iota(jnp.int32, (n_heads, n_heads * width), 1)
    return jnp.where(c // width == r, 1.0, 0.0).astype(bf16)


def _prenorm_body(xp_ref, xs_ref, nw_ref, wdt_ref, dtb_ref, hnp_ref, hns_ref, dtg_ref, dtt_ref, dtsg_ref):
    i = pl.program_id(0)
    w_hi, w_lo = _split2(wdt_ref[0:N_SSM_HEADS, :])

    def dt_rows(hn, grouped_ref):
        a = hn.astype(bf16)
        raw_t = _dot_nt(w_hi, a) + _dot_nt(w_lo, a)
        dt_t = _softplus(raw_t + dtb_ref[...])
        dt = dt_t.T
        for g in range(N_SSM_GROUPS):
            grouped_ref[g] = dt[:, g * HEADS_PER_GROUP:(g + 1) * HEADS_PER_GROUP]
        return dt_t

    @pl.when(i < NTP)
    def _():
        hn = _rms(xp_ref[...], nw_ref[...])
        hnp_ref[...] = hn.astype(bf16)
        dtt_ref[...] = dt_rows(hn, dtg_ref)

    @pl.when(i == NTP)
    def _():
        hn = _rms(xs_ref[...], nw_ref[...])
        hns_ref[...] = hn
        dt_rows(hn, dtsg_ref)


def _prenorm(xp, xs, norm_w, w_in_t, dt_bias):
    pidx = lambda i: (jnp.minimum(i, NTP - 1), 0)
    const = lambda i: (0, 0)
    return pl.pallas_call(
        _prenorm_body,
        grid=(NTP + 1,),
        in_specs=[
            pl.BlockSpec((TP, D_MODEL), pidx),
            pl.BlockSpec((NS, D_MODEL), const),
            pl.BlockSpec((1, D_MODEL), const),
            pl.BlockSpec((128, D_MODEL), lambda i: (OFF_DT // 128, 0)),
            pl.BlockSpec((N_SSM_HEADS, 1), const),
        ],
        out_specs=[
            pl.BlockSpec((TP, D_MODEL), pidx),
            pl.BlockSpec((NS, D_MODEL), const),
            pl.BlockSpec((N_SSM_GROUPS, TP, HEADS_PER_GROUP), lambda i: (0, jnp.minimum(i, NTP - 1), 0)),
            pl.BlockSpec((N_SSM_HEADS, TP), lambda i: (0, jnp.minimum(i, NTP - 1))),
            pl.BlockSpec((N_SSM_GROUPS, NS, HEADS_PER_GROUP), lambda i: (0, 0, 0)),
        ],
        out_shape=[
            jax.ShapeDtypeStruct((NP, D_MODEL), bf16),
            jax.ShapeDtypeStruct((NS, D_MODEL), f32),
            jax.ShapeDtypeStruct((N_SSM_GROUPS, NP, HEADS_PER_GROUP), f32),
            jax.ShapeDtypeStruct((N_SSM_HEADS, NP), f32),
            jax.ShapeDtypeStruct((N_SSM_GROUPS, NS, HEADS_PER_GROUP), f32),
        ],
        compiler_params=_cparams(("arbitrary",)),
        name="prenorm_dt",
    )(xp, xs, norm_w.reshape(1, D_MODEL), w_in_t, dt_bias.reshape(-1, 1))


CAST_CHUNK = 256


def _rows_matmul_body(*refs, n_a, n_w, n_row, n_col, n_out, dots, epilogue, shifts, transposed):
    pos = 0
    a_p = refs[pos:pos + n_a]; pos += n_a
    a_s = refs[pos:pos + n_a]; pos += n_a
    w = refs[pos:pos + n_w]; pos += n_w
    n_tail = sum(1 for s in shifts if s)
    tails = iter(refs[pos:pos + n_tail]); pos += n_tail
    w_tail = [next(tails) if s else None for s in shifts]
    row_p = refs[pos:pos + n_row]; pos += n_row
    row_s = refs[pos:pos + n_row]; pos += n_row
    col = refs[pos:pos + n_col]; pos += n_col
    out_p = refs[pos:pos + n_out]; pos += n_out
    out_s = refs[pos:pos + n_out]; pos += n_out
    wb = refs[pos:pos + n_w]
    i = pl.program_id(1)

    @pl.when(i == 0)
    def _():
        for k in range(n_w):
            if transposed[k]:
                tn = w[k].shape[0]
                for c in range(tn // CAST_CHUNK):
                    r0 = c * CAST_CHUNK + shifts[k]
                    if r0 + CAST_CHUNK <= tn:
                        blk = w[k][r0:r0 + CAST_CHUNK, :]
                    else:
                        blk = jnp.concatenate([w[k][r0:tn, :], w_tail[k][0:r0 + CAST_CHUNK - tn, :]], axis=0)
                    wb[k][:, c * CAST_CHUNK:(c + 1) * CAST_CHUNK] = blk.T.astype(bf16)
                continue

            def cast(c, carry, k=k):
                r0 = pl.multiple_of(c * CAST_CHUNK, CAST_CHUNK)
                blk = w[k][pl.ds(r0, CAST_CHUNK), :]
                if shifts[k]:
                    blk = jnp.concatenate([blk[:, shifts[k]:], w_tail[k][pl.ds(r0, CAST_CHUNK), 0:shifts[k]]], axis=1)
                wb[k][pl.ds(r0, CAST_CHUNK), :] = blk.astype(bf16)
                return carry

            lax.fori_loop(0, w[k].shape[0] // CAST_CHUNK, cast, 0)

    def compute(a, row, out):
        rs = [_dot(a[ai][...].astype(bf16), wb[wi][...]) for ai, wi in dots]
        res = epilogue(rs, [r[...].astype(f32) for r in row], [c[...] for c in col])
        for o, v in zip(out, res):
            o[...] = v.astype(o.dtype)

    @pl.when(i == 0)
    def _():
        compute(a_s, row_s, out_s)

    @pl.when(i > 0)
    def _():
        compute(a_p, row_p, out_p)


def _rows_matmul(name, a_pairs, weights, dots, epilogue, out_dtypes, tn, n_panels,
                 row_pairs=(), cols=(), tp=TP, single_buffer_weights=False):
    n_a, n_w, n_row, n_col, n_out = len(a_pairs), len(weights), len(row_pairs), len(cols), len(out_dtypes)
    ptile = lambda i: jnp.maximum(i - 1, 0)
    pidx = lambda n, i: (ptile(i), 0)
    sidx = lambda n, i: (0, 0)
    in_specs, args = [], []
    for ap, _ in a_pairs:
        in_specs.append(pl.BlockSpec((tp, ap.shape[1]), pidx)); args.append(ap)
    for _, as_ in a_pairs:
        in_specs.append(pl.BlockSpec((NS, as_.shape[1]), sidx)); args.append(as_)
    w_mode = dict(pipeline_mode=pl.Buffered(1)) if single_buffer_weights else {}
    kdim = lambda wk, tr: wk.shape[1] if tr else wk.shape[0]
    for wk, off, _, tr in weights:
        panel = functools.partial(lambda n, i, off: (0, n + off), off=off)
        spec = (pl.BlockSpec((tn, kdim(wk, tr)), lambda n, i, panel=panel: panel(n, i)[::-1], **w_mode) if tr
                else pl.BlockSpec((kdim(wk, tr), tn), panel, **w_mode))
        in_specs.append(spec); args.append(wk)
    for wk, off, shift, tr in weights:
        if shift:
            after = functools.partial(lambda n, i, off: (0, (n + off + 1) * (tn // 128)), off=off)
            spec = (pl.BlockSpec((128, kdim(wk, tr)), lambda n, i, after=after: after(n, i)[::-1], **w_mode) if tr
                    else pl.BlockSpec((kdim(wk, tr), 128), after, **w_mode))
            in_specs.append(spec); args.append(wk)
    out_p_idx = lambda n, i: (ptile(i), n)
    out_s_idx = lambda n, i: (0, n)
    for rp, _, off in row_pairs:
        in_specs.append(pl.BlockSpec((tp, tn), functools.partial(lambda n, i, off: (ptile(i), n + off), off=off)))
        args.append(rp)
    for _, rs, off in row_pairs:
        in_specs.append(pl.BlockSpec((NS, tn), functools.partial(lambda n, i, off: (0, n + off), off=off)))
        args.append(rs)
    for c in cols:
        in_specs.append(pl.BlockSpec((1, tn), lambda n, i: (0, n))); args.append(c)
    width = n_panels * tn
    out_specs = ([pl.BlockSpec((tp, tn), out_p_idx)] * n_out + [pl.BlockSpec((NS, tn), out_s_idx)] * n_out)
    out_shape = ([jax.ShapeDtypeStruct((NP, width), dp) for dp, _ in out_dtypes]
                 + [jax.ShapeDtypeStruct((NS, width), ds) for _, ds in out_dtypes])
    scratch = [pltpu.VMEM((kdim(wk, tr), tn), bf16) for wk, _, _, tr in weights]
    body = functools.partial(_rows_matmul_body, n_a=n_a, n_w=n_w, n_row=n_row, n_col=n_col, n_out=n_out,
                             dots=tuple(dots), epilogue=epilogue,
                             shifts=tuple(s for _, _, s, _ in weights),
                             transposed=tuple(tr for _, _, _, tr in weights))
    res = pl.pallas_call(
        body,
        grid=(n_panels, NP // tp + 1),
        in_specs=in_specs,
        out_specs=out_specs,
        out_shape=out_shape,
        scratch_shapes=scratch,
        compiler_params=_cparams(("arbitrary", "arbitrary")),
        name=name,
    )(*args)
    return [(res[k], res[n_out + k]) for k in range(n_out)]


def _rope_tables(pos):
    half = HEAD_DIM // 2
    inv = ROPE_THETA ** (-jnp.arange(half, dtype=f32) / half)
    ang = pos.astype(f32)[:, None] * inv[None, :]
    cos, sin = jnp.cos(ang), jnp.sin(ang)
    return jnp.concatenate([cos, cos, cos, cos], axis=1), jnp.concatenate([-sin, sin, -sin, sin], axis=1)


def _rope_chunk(x, cos, sin, first_half):
    from_right = pltpu.roll(x, 96, 1)
    from_left = pltpu.roll(x, 32, 1)
    return x * cos + jnp.where(first_half, from_right, from_left) * sin


def _rope(x, cos, sin):
    rows, width = x.shape
    lane = lax.broadcasted_iota(jnp.int32, (rows, 128), 1)
    first_half = (lane % HEAD_DIM) < (HEAD_DIM // 2)
    return [_rope_chunk(x[:, c * 128:(c + 1) * 128], cos, sin, first_half) for c in range(width // 128)]


def _attn_prompt_body(sink_ref, q_ref, kv_ref, cos_ref, sin_ref, cost_ref, sint_ref, bias_ref,
                      o_ref, wk_ref, kbuf, vbuf_t):
    n = pl.program_id(1)
    cur = n % 2
    prev = 1 - cur

    @pl.when(n == 0)
    def _():
        kbuf[1] = jnp.zeros((WINDOW, KV_WIDTH), bf16)
        vbuf_t[1] = jnp.zeros((KV_WIDTH, WINDOW), bf16)

    kc = _rope(kv_ref[:, 0:KV_WIDTH].astype(f32), cos_ref[...], sin_ref[...])
    for c, v in enumerate(kc):
        kbuf[cur, :, c * 128:(c + 1) * 128] = v.astype(bf16)
    vbuf_t[cur] = kv_ref[:, KV_WIDTH:2 * KV_WIDTH].astype(f32).T.astype(bf16)

    @pl.when(n == N_CHUNKS - 1)
    def _():
        for c, v in enumerate(kc):
            wk_ref[:, c * 128:(c + 1) * 128] = v

    qt = q_ref[...].astype(f32).T
    cost, sint = cost_ref[...], sint_ref[...]
    bias = bias_ref[...]
    scale = HEAD_DIM ** -0.5
    half = HEAD_DIM // 2
    pad = jnp.zeros((HEAD_DIM, CHUNK), f32)
    heads_per_pair = 2 * GQA_GROUP
    heads = []
    for pr in range(N_KV_HEADS // 2):
        qcols = []
        for hh in range(heads_per_pair):
            h = pr * heads_per_pair + hh
            x1 = qt[h * HEAD_DIM:h * HEAD_DIM + half, :]
            x2 = qt[h * HEAD_DIM + half:(h + 1) * HEAD_DIM, :]
            r1 = (x1 * cost - x2 * sint) * scale
            r2 = (x2 * cost + x1 * sint) * scale
            qcols.append(jnp.concatenate([r1, r2, pad] if hh < GQA_GROUP else [pad, r1, r2], axis=0))
        qw = jnp.concatenate(qcols, axis=1).astype(bf16)
        lanes = slice(pr * 128, (pr + 1) * 128)
        kpair = jnp.concatenate([kbuf[prev, :, lanes], kbuf[cur, :, lanes]], axis=0)
        st = _dot(kpair, qw)
        st = st + jnp.concatenate([bias] * heads_per_pair, axis=1)
        sink = sink_ref[:, pr * heads_per_pair * CHUNK:(pr + 1) * heads_per_pair * CHUNK]
        m = jnp.maximum(jnp.max(st, axis=0, keepdims=True), sink)
        p = jnp.exp(st - m)
        inv = 1.0 / (jnp.sum(p, axis=0, keepdims=True) + jnp.exp(sink - m))
        pb = p.astype(bf16)
        for k in range(2):
            g = 2 * pr + k
            cols = slice(k * GQA_GROUP * CHUNK, (k + 1) * GQA_GROUP * CHUNK)
            dims = slice(g * HEAD_DIM, (g + 1) * HEAD_DIM)
            vgt = jnp.concatenate([vbuf_t[prev, dims, :], vbuf_t[cur, dims, :]], axis=1)
            og = _dot(vgt, pb[:, cols]) * inv[:, cols]
            heads += [og[:, j * CHUNK:(j + 1) * CHUNK] for j in range(GQA_GROUP)]
    o_ref[...] = jnp.concatenate(heads, axis=0).T.astype(o_ref.dtype)


def _band_bias():
    qi = np.arange(WINDOW)[None, :]
    kj = np.arange(2 * WINDOW)[:, None]
    cur = (kj >= WINDOW) & (kj - WINDOW <= qi)
    prev = (kj < WINDOW) & (kj > qi)
    neg = np.float32(-np.inf)
    b0 = np.where(cur, np.float32(0), neg)
    b1 = np.where(cur | prev, np.float32(0), neg)
    return jnp.asarray(np.stack([b0, b1]).astype(np.float32))


def _attn_prompt(proj_p, sinks):
    pos = jnp.arange(SEQ, dtype=jnp.int32)
    cos, sin = _rope_tables(pos)
    half = HEAD_DIM // 2
    cos_t, sin_t = cos[:, :half].T, sin[:, half:HEAD_DIM].T
    row = lambda b, n: b * N_CHUNKS + n
    return pl.pallas_call(
        _attn_prompt_body,
        grid=(BATCH, N_CHUNKS),
        in_specs=[
            pl.BlockSpec((1, N_HEADS * CHUNK), lambda b, n: (0, 0)),
            pl.BlockSpec((CHUNK, ATTN_WIDTH), lambda b, n: (row(b, n), 0)),
            pl.BlockSpec((CHUNK, 2 * KV_WIDTH), lambda b, n: (row(b, n), OFF_K // (2 * KV_WIDTH))),
            pl.BlockSpec((CHUNK, 128), lambda b, n: (n, 0)),
            pl.BlockSpec((CHUNK, 128), lambda b, n: (n, 0)),
            pl.BlockSpec((half, CHUNK), lambda b, n: (0, n)),
            pl.BlockSpec((half, CHUNK), lambda b, n: (0, n)),
            pl.BlockSpec((None, 2 * WINDOW, CHUNK), lambda b, n: (jnp.minimum(n, 1), 0, 0)),
        ],
        out_specs=[
            pl.BlockSpec((CHUNK, ATTN_WIDTH), lambda b, n: (row(b, n), 0)),
            pl.BlockSpec((None, WINDOW, KV_WIDTH), lambda b, n: (b, 0, 0)),
        ],
        out_shape=[
            jax.ShapeDtypeStruct((NP, ATTN_WIDTH), bf16),
            jax.ShapeDtypeStruct((BATCH, WINDOW, KV_WIDTH), f32),
        ],
        scratch_shapes=[pltpu.VMEM((2, WINDOW, KV_WIDTH), bf16), pltpu.VMEM((2, KV_WIDTH, WINDOW), bf16)],
        compiler_params=_cparams(("arbitrary", "arbitrary")),
        name="attn_prompt",
    )(jnp.repeat(sinks, CHUNK).reshape(1, N_HEADS * CHUNK), proj_p, proj_p, cos, sin, cos_t, sin_t, _band_bias())


SB = 8


def _attn_sample_body(q_ref, k_ref, v_ref, kt_ref, vt_ref, cos_ref, sin_ref, sink_ref, o_ref, kto_ref, vto_ref):
    cos, sin = cos_ref[...], sin_ref[...]
    scale = HEAD_DIM ** -0.5
    cw = kt_ref.shape[2]
    is_new = lax.broadcasted_iota(jnp.int32, (KV_WIDTH, cw), 1) == cw - 1
    head = lax.broadcasted_iota(jnp.int32, (N_HEADS, KV_WIDTH), 0)
    col = lax.broadcasted_iota(jnp.int32, (N_HEADS, KV_WIDTH), 1)
    own = (col // HEAD_DIM) == (head // GQA_GROUP)
    sink = sink_ref[...]
    piece_row = lax.broadcasted_iota(jnp.int32, (3 * SB, SB * cw), 0) % SB
    lane = lax.broadcasted_iota(jnp.int32, (3 * SB, SB * cw), 1)
    place = jnp.where(lane == piece_row * cw + cw - 1, 1.0, 0.0).astype(bf16)

    def as_last_columns(rows):
        return _dot_tn(jnp.concatenate(_split3(rows), axis=0), place)

    kn_cols = as_last_columns(jnp.concatenate(_rope(k_ref[...], cos, sin), axis=1))
    vn_cols = as_last_columns(v_ref[...])
    for b in range(SB):
        def appended(t_ref, new_cols):
            return jnp.where(is_new, new_cols[:, b * cw:(b + 1) * cw], pltpu.roll(t_ref[b], cw - 1, 1))
        ktn = appended(kt_ref, kn_cols)
        vtn = appended(vt_ref, vn_cols)
        kto_ref[b] = ktn
        vto_ref[b] = vtn
        qr = jnp.concatenate(_rope(q_ref[b], cos, sin), axis=1) * scale
        s = _dot(jnp.where(own, qr, 0.0).astype(bf16), ktn.astype(bf16))
        m = jnp.maximum(jnp.max(s, axis=1, keepdims=True), sink)
        p = jnp.exp(s - m)
        inv = 1.0 / (jnp.sum(p, axis=1, keepdims=True) + jnp.exp(sink - m))
        o = jnp.where(own, _dot_nt((p * inv).astype(bf16), vtn.astype(bf16)), 0.0)
        o8 = o[0:8] + o[8:16] + o[16:24] + o[24:32]
        o_ref[b] = (o8 + pltpu.roll(o8, GQA_GROUP, 0))[0:GQA_GROUP]


def _attn_sample(proj_s, cache_k, cache_v, cos, sin, sinks):
    cw = cache_k.shape[1]
    assert cw == WINDOW, "the new token's window is exactly the cache minus its oldest entry"
    as_t = lambda c: jnp.transpose(c, (0, 2, 3, 1)).reshape(NS, KV_WIDTH, cw)
    q_rep = jnp.tile(proj_s[:, OFF_Q:OFF_K].reshape(NS, N_HEADS, HEAD_DIM), (1, 1, N_KV_HEADS))
    rows = lambda i: (i, 0)
    rows3 = lambda i: (i, 0, 0)
    const = lambda i: (0, 0)
    o, kto, vto = pl.pallas_call(
        _attn_sample_body,
        grid=(NS // SB,),
        in_specs=[
            pl.BlockSpec((SB, N_HEADS, KV_WIDTH), rows3),
            pl.BlockSpec((SB, KV_WIDTH), lambda i: (i, OFF_K // KV_WIDTH)),
            pl.BlockSpec((SB, KV_WIDTH), lambda i: (i, OFF_V // KV_WIDTH)),
            pl.BlockSpec((SB, KV_WIDTH, cw), rows3),
            pl.BlockSpec((SB, KV_WIDTH, cw), rows3),
            pl.BlockSpec((1, 128), const),
            pl.BlockSpec((1, 128), const),
            pl.BlockSpec((N_HEADS, 1), const),
        ],
        out_specs=[
            pl.BlockSpec((SB, GQA_GROUP, KV_WIDTH), rows3),
            pl.BlockSpec((SB, KV_WIDTH, cw), rows3),
            pl.BlockSpec((SB, KV_WIDTH, cw), rows3),
        ],
        out_shape=[
            jax.ShapeDtypeStruct((NS, GQA_GROUP, KV_WIDTH), f32),
            jax.ShapeDtypeStruct((NS, KV_WIDTH, cw), f32),
            jax.ShapeDtypeStruct((NS, KV_WIDTH, cw), f32),
        ],
        compiler_params=_cparams(("arbitrary",)),
        name="attn_sample",
    )(q_rep, proj_s, proj_s, as_t(cache_k), as_t(cache_v), cos, sin, sinks.reshape(N_HEADS, 1))
    attn = o.reshape(NS, GQA_GROUP, N_KV_HEADS, HEAD_DIM).transpose(0, 2, 1, 3).reshape(NS, ATTN_WIDTH)
    back = lambda t: jnp.transpose(t.reshape(NS, N_KV_HEADS, HEAD_DIM, cw), (0, 3, 1, 2))
    return attn, back(kto), back(vto)


GPS = 2
XW = GPS * GROUP_WIDTH
BW = GPS * D_STATE
XBCW = XW + 2 * BW
CPS = 4
TAIL = 16


def _ssd_prompt_body(x_ref, b_ref, c_ref, z_ref, dt_ref, dtt_ref, cwx_ref, cwb_ref, cwc_ref,
                     cbx_ref, cbb_ref, cbc_ref, alogt_ref, dsk_ref, nw_ref,
                     y_ref, st_ref, hst, tails):
    c = pl.program_id(2)
    cur = c % 2
    prev = 1 - cur

    @pl.when(c == 0)
    def _():
        hst[1] = jnp.zeros((GPS, D_STATE, GROUP_WIDTH), f32)
        tails[1] = jnp.zeros((TAIL, XBCW), bf16)

    ti = lax.broadcasted_iota(jnp.int32, (CONV_W * CHUNK, 2 * CHUNK), 0)
    si = lax.broadcasted_iota(jnp.int32, (CONV_W * CHUNK, 2 * CHUNK), 1)
    pick = jnp.where(si == CHUNK + (ti % CHUNK) - (ti // CHUNK), 1.0, 0.0).astype(bf16)
    w = jnp.concatenate([cwx_ref[...], cwb_ref[...], cwc_ref[...]], axis=1)
    bias = jnp.concatenate([cbx_ref[...], cbb_ref[...], cbc_ref[...]], axis=1)
    li = lax.broadcasted_iota(jnp.int32, (CHUNK, CHUNK), 0)
    si = lax.broadcasted_iota(jnp.int32, (CHUNK, CHUNK), 1)
    causal = li >= si
    trit = jnp.where(li <= si, 1.0, 0.0).astype(bf16)
    sel = _head_selector(HEADS_PER_GROUP, SSM_HEAD_DIM)

    def expand(v):
        hi = v.astype(bf16)
        lo = (v - hi.astype(f32)).astype(bf16)
        return _dot(hi, sel) + _dot(lo, sel)

    expand_row = lambda v: _dot_f32_lhs(jnp.broadcast_to(v, (8, HEADS_PER_GROUP)), sel)[0:1, :]
    lane = lax.broadcasted_iota(jnp.int32, (CHUNK, 128), 1)
    zero = jnp.zeros((CHUNK, 128), bf16)
    log2e = 1.4426950408889634

    tail = tails[prev]
    h = [hst[prev, k] for k in range(GPS)]
    for cc in range(CPS):
        rows = slice(cc * CHUNK, (cc + 1) * CHUNK)
        xbc = jnp.concatenate([x_ref[rows, :], b_ref[rows, :], c_ref[rows, :]], axis=1)
        stacked = jnp.concatenate([jnp.zeros((CHUNK - TAIL, XBCW), bf16), tail, xbc], axis=0)
        tail = xbc[CHUNK - TAIL:, :]
        taps = _dot(pick, stacked)
        acc = bias
        for j in range(CONV_W):
            acc = acc + w[CONV_W - 1 - j:CONV_W - j, :] * taps[j * CHUNK:(j + 1) * CHUNK, :]
        act = _silu(acc)

        for k in range(GPS):
            xs = act[:, k * GROUP_WIDTH:(k + 1) * GROUP_WIDTH]
            bmb = act[:, XW + k * D_STATE:XW + (k + 1) * D_STATE].astype(bf16)
            cmb = act[:, XW + BW + k * D_STATE:XW + BW + (k + 1) * D_STATE].astype(bf16)
            dtt = dtt_ref[k * HEADS_PER_GROUP:(k + 1) * HEADS_PER_GROUP, rows]
            dat = dtt * (-jnp.exp(alogt_ref[k]))
            acumt = _dot_f32_lhs(dat, trit)
            acum = acumt.T
            a_last = acum[CHUNK - 1:CHUNK, :]
            acum2 = acum * log2e
            src2 = acumt * log2e - jnp.log2(dtt)

            xb = xs.astype(bf16)
            xdec = (xs * expand(dt_ref[k, rows, :] * jnp.exp(a_last - acum))).astype(bf16)

            cb = _dot_nt(cmb, bmb)
            pairs = []
            for pr in range(HEADS_PER_GROUP // 2):
                ms = []
                for r in (2 * pr, 2 * pr + 1):
                    seg = acum2[:, r:r + 1] - src2[r:r + 1, :]
                    ms.append((cb * jnp.exp2(jnp.where(causal, seg, -jnp.inf))).astype(bf16))
                xp = xb[:, pr * 128:(pr + 1) * 128]
                x0 = jnp.where(lane < SSM_HEAD_DIM, xp, zero)
                x1 = jnp.where(lane < SSM_HEAD_DIM, zero, xp)
                pairs.append(_dot(jnp.concatenate(ms, axis=1), jnp.concatenate([x0, x1], axis=0)))
            y = jnp.concatenate(pairs, axis=1)

            h_prev = h[k]
            y = y + _dot(cmb, h_prev.astype(bf16)) * expand(jnp.exp2(acum2))
            y = y + expand_row(dsk_ref[k]) * xs

            cols = slice(k * GROUP_WIDTH, (k + 1) * GROUP_WIDTH)
            gz = y * _silu(z_ref[rows, cols].astype(f32))
            y_ref[rows, cols] = _rms(gz, nw_ref[:, cols]).astype(y_ref.dtype)

            h_new = h_prev * expand_row(jnp.exp(a_last)) + _dot_tn(bmb, xdec)
            h[k] = h_new

    tails[cur] = tail
    for k in range(GPS):
        hst[cur, k] = h[k]

    @pl.when(c == N_CHUNKS // CPS - 1)
    def _():
        for k in range(GPS):
            st_ref[k] = hst[cur, k].T


def _ssd_prompt(proj_p, dtg, dtt, conv_w, conv_b, a_log, d_skip, ssm_norm):
    steps = N_CHUNKS // CPS
    rows = CPS * CHUNK
    row = lambda b, g, c: b * steps + c
    xoff = OFF_XBC // XW
    boff = (OFF_XBC + D_INNER) // BW
    coff = boff + N_SSM_GROUPS // GPS
    zoff = OFF_Z // XW
    wboff = D_INNER // BW
    wcoff = wboff + N_SSM_GROUPS // GPS
    alog_t = a_log.reshape(N_SSM_GROUPS, HEADS_PER_GROUP, 1)
    dsk_g = d_skip.reshape(N_SSM_GROUPS, 1, HEADS_PER_GROUP)
    conv_b = conv_b.reshape(1, CONV_DIM)
    grp = lambda b, g, c: (g, 0, 0)
    return pl.pallas_call(
        _ssd_prompt_body,
        grid=(BATCH, N_SSM_GROUPS // GPS, steps),
        in_specs=[
            pl.BlockSpec((rows, XW), lambda b, g, c: (row(b, g, c), xoff + g)),
            pl.BlockSpec((rows, BW), lambda b, g, c: (row(b, g, c), boff + g)),
            pl.BlockSpec((rows, BW), lambda b, g, c: (row(b, g, c), coff + g)),
            pl.BlockSpec((rows, XW), lambda b, g, c: (row(b, g, c), zoff + g)),
            pl.BlockSpec((GPS, rows, HEADS_PER_GROUP), lambda b, g, c: (g, row(b, g, c), 0)),
            pl.BlockSpec((GPS * HEADS_PER_GROUP, rows), lambda b, g, c: (g, row(b, g, c))),
            pl.BlockSpec((CONV_W, XW), lambda b, g, c: (0, g)),
            pl.BlockSpec((CONV_W, BW), lambda b, g, c: (0, wboff + g)),
            pl.BlockSpec((CONV_W, BW), lambda b, g, c: (0, wcoff + g)),
            pl.BlockSpec((1, XW), lambda b, g, c: (0, g)),
            pl.BlockSpec((1, BW), lambda b, g, c: (0, wboff + g)),
            pl.BlockSpec((1, BW), lambda b, g, c: (0, wcoff + g)),
            pl.BlockSpec((GPS, HEADS_PER_GROUP, 1), grp),
            pl.BlockSpec((GPS, 1, HEADS_PER_GROUP), grp),
            pl.BlockSpec((1, XW), lambda b, g, c: (0, g)),
        ],
        out_specs=[
            pl.BlockSpec((rows, XW), lambda b, g, c: (row(b, g, c), g)),
            pl.BlockSpec((None, GPS, GROUP_WIDTH, D_STATE), lambda b, g, c: (b, g, 0, 0)),
        ],
        out_shape=[
            jax.ShapeDtypeStruct((NP, D_INNER), bf16),
            jax.ShapeDtypeStruct((BATCH, N_SSM_GROUPS, GROUP_WIDTH, D_STATE), f32),
        ],
        scratch_shapes=[
            pltpu.VMEM((2, GPS, D_STATE, GROUP_WIDTH), f32),
            pltpu.VMEM((2, TAIL, XBCW), bf16),
        ],
        compiler_params=_cparams(("arbitrary", "arbitrary", "arbitrary")),
        name="ssd_prompt",
    )(proj_p, proj_p, proj_p, proj_p, dtg, dtt, conv_w, conv_w, conv_w, conv_b, conv_b, conv_b,
      alog_t, dsk_g, ssm_norm.reshape(1, D_INNER))


SQ = 16
GSS = 2


def _ssd_sample_body(x_ref, b_ref, c_ref, sx_ref, sb_ref, sc_ref, z_ref, dt_ref, h_ref,
                     cwx_ref, cwb_ref, cwc_ref, cbx_ref, cbb_ref, cbc_ref, alog_ref, dsk_ref, nw_ref,
                     y_ref, ho_ref):
    def conv(new_ref, st_ref, w_ref, bias_ref):
        w = w_ref[...]
        acc = bias_ref[...] + w[CONV_W - 1:CONV_W, :] * new_ref[...]
        for j in range(CONV_W - 1):
            acc = acc + w[j:j + 1, :] * st_ref[j]
        return _silu(acc)

    xs_all = conv(x_ref, sx_ref, cwx_ref, cbx_ref)
    bm_all = conv(b_ref, sb_ref, cwb_ref, cbb_ref)
    cm_all = conv(c_ref, sc_ref, cwc_ref, cbc_ref)
    zs_all = _silu(z_ref[...])
    sel = _head_selector(HEADS_PER_GROUP, SSM_HEAD_DIM)
    exact = lambda v: v.astype(bf16).astype(f32)
    zeros = jnp.zeros((SQ, D_STATE), f32)
    ones = jnp.ones((SQ, D_STATE), f32)
    wide = (6 * SQ, SQ * 2 * D_STATE)
    row_seq = lax.broadcasted_iota(jnp.int32, wide, 0) % SQ
    col_seq = lax.broadcasted_iota(jnp.int32, wide, 1) // (2 * D_STATE)
    c_row = lax.broadcasted_iota(jnp.int32, (SQ, SQ * D_STATE), 0)
    c_col = lax.broadcasted_iota(jnp.int32, (SQ, SQ * D_STATE), 1) // D_STATE

    for k in range(GSS):
        xs = xs_all[:, k * GROUP_WIDTH:(k + 1) * GROUP_WIDTH]
        bm = bm_all[:, k * D_STATE:(k + 1) * D_STATE]
        cm = cm_all[:, k * D_STATE:(k + 1) * D_STATE]
        dt = dt_ref[k]
        decay = jnp.exp(dt * (-jnp.exp(alog_ref[k])))
        xdt = xs * _dot_f32_lhs(dt, sel)
        dec = _dot_f32_lhs(decay, sel)

        x_hi = exact(xdt)
        d_hi = exact(dec)
        d_mid = exact(dec - d_hi)
        b_hi = exact(bm)
        lhs = jnp.concatenate([x_hi, x_hi, xdt - x_hi, d_hi, d_mid, dec - d_hi - d_mid], axis=0)
        lhs_t = lhs.T.astype(bf16)
        outer_rows = [jnp.concatenate([v, zeros], axis=1) for v in (b_hi, bm - b_hi, b_hi)]
        rhs = jnp.concatenate(outer_rows + [jnp.concatenate([zeros, ones], axis=1)] * 3, axis=0)
        rhs_wide = jnp.where(row_seq == col_seq, jnp.concatenate([rhs] * SQ, axis=1), 0.0).astype(bf16)
        both = _dot(lhs_t, rhs_wide)
        h_bf = []
        rows = slice(k * GROUP_WIDTH, (k + 1) * GROUP_WIDTH)
        for s in range(SQ):
            upd = both[:, s * 2 * D_STATE:(s + 1) * 2 * D_STATE]
            h_new = h_ref[s, rows, :] * upd[:, D_STATE:] + upd[:, 0:D_STATE]
            ho_ref[s, rows, :] = h_new
            h_bf.append(h_new.astype(bf16))
        c_diag = jnp.where(c_row == c_col, jnp.concatenate([cm] * SQ, axis=1), 0.0).astype(bf16)
        y = _dot_nt(c_diag, jnp.concatenate(h_bf, axis=1))
        dsk = _dot_f32_lhs(jnp.broadcast_to(dsk_ref[k], (SQ, HEADS_PER_GROUP)), sel)
        gz = (y + dsk * xs) * zs_all[:, rows]
        y_ref[:, rows] = _rms(gz, nw_ref[:, rows])


def _ssd_sample(proj_s, dtsg, state_conv, state_ssm, conv_w, conv_b, a_log, d_skip, ssm_norm):
    xw, bw = GSS * GROUP_WIDTH, GSS * D_STATE
    xoff = OFF_XBC // xw
    boff = (OFF_XBC + D_INNER) // bw
    coff = boff + N_SSM_GROUPS // GSS
    zoff = OFF_Z // xw
    wboff = D_INNER // bw
    wcoff = wboff + N_SSM_GROUPS // GSS
    conv_b = conv_b.reshape(1, CONV_DIM)
    grp = lambda i, g: (g, 0, 0)
    return pl.pallas_call(
        _ssd_sample_body,
        grid=(NS // SQ, N_SSM_GROUPS // GSS),
        in_specs=[
            pl.BlockSpec((SQ, xw), lambda i, g: (i, xoff + g)),
            pl.BlockSpec((SQ, bw), lambda i, g: (i, boff + g)),
            pl.BlockSpec((SQ, bw), lambda i, g: (i, coff + g)),
            pl.BlockSpec((CONV_W - 1, SQ, xw), lambda i, g: (0, i, g)),
            pl.BlockSpec((CONV_W - 1, SQ, bw), lambda i, g: (0, i, wboff + g)),
            pl.BlockSpec((CONV_W - 1, SQ, bw), lambda i, g: (0, i, wcoff + g)),
            pl.BlockSpec((SQ, xw), lambda i, g: (i, zoff + g)),
            pl.BlockSpec((GSS, SQ, HEADS_PER_GROUP), lambda i, g: (g, i, 0)),
            pl.BlockSpec((SQ, xw, D_STATE), lambda i, g: (i, g, 0)),
            pl.BlockSpec((CONV_W, xw), lambda i, g: (0, g)),
            pl.BlockSpec((CONV_W, bw), lambda i, g: (0, wboff + g)),
            pl.BlockSpec((CONV_W, bw), lambda i, g: (0, wcoff + g)),
            pl.BlockSpec((1, xw), lambda i, g: (0, g)),
            pl.BlockSpec((1, bw), lambda i, g: (0, wboff + g)),
            pl.BlockSpec((1, bw), lambda i, g: (0, wcoff + g)),
            pl.BlockSpec((GSS, 1, HEADS_PER_GROUP), grp),
            pl.BlockSpec((GSS, 1, HEADS_PER_GROUP), grp),
            pl.BlockSpec((1, xw), lambda i, g: (0, g)),
        ],
        out_specs=[
            pl.BlockSpec((SQ, xw), lambda i, g: (i, g)),
            pl.BlockSpec((SQ, xw, D_STATE), lambda i, g: (i, g, 0)),
        ],
        out_shape=[
            jax.ShapeDtypeStruct((NS, D_INNER), f32),
            jax.ShapeDtypeStruct((NS, D_INNER, D_STATE), f32),
        ],
        compiler_params=_cparams(("arbitrary", "arbitrary")),
        name="ssd_sample",
    )(proj_s, proj_s, proj_s, state_conv, state_conv, state_conv, proj_s, dtsg,
      state_ssm.reshape(NS, D_INNER, D_STATE), conv_w, conv_w, conv_w, conv_b, conv_b, conv_b,
      a_log.reshape(N_SSM_GROUPS, 1, HEADS_PER_GROUP), d_skip.reshape(N_SSM_GROUPS, 1, HEADS_PER_GROUP),
      ssm_norm.reshape(1, D_INNER))


def _resid_norm_body(mp, ms, xp, xs, w, op, os_):
    i = pl.program_id(0)

    @pl.when(i < NTR)
    def _():
        op[...] = xp[...] + _rms(mp[...].astype(f32), w[...])

    @pl.when(i == NTR)
    def _():
        os_[...] = xs[...] + _rms(ms[...].astype(f32), w[...])


def _resid_norm(name, m_pair, x_pair, w_post):
    pidx = lambda i: (jnp.minimum(i, NTR - 1), 0)
    const = lambda i: (0, 0)
    return pl.pallas_call(
        _resid_norm_body,
        grid=(NTR + 1,),
        in_specs=[
            pl.BlockSpec((TR, D_MODEL), pidx), pl.BlockSpec((NS, D_MODEL), const),
            pl.BlockSpec((TR, D_MODEL), pidx), pl.BlockSpec((NS, D_MODEL), const),
            pl.BlockSpec((1, D_MODEL), const),
        ],
        out_specs=[pl.BlockSpec((TR, D_MODEL), pidx), pl.BlockSpec((NS, D_MODEL), const)],
        out_shape=[jax.ShapeDtypeStruct((NP, D_MODEL), f32), jax.ShapeDtypeStruct((NS, D_MODEL), f32)],
        compiler_params=_cparams(("arbitrary",)),
        name=name,
    )(m_pair[0], m_pair[1], x_pair[0], x_pair[1], w_post.reshape(1, D_MODEL))


def kernel(x_prompt, x_sample, cache_win_k, cache_win_v, state_conv, state_ssm, norm_mix_pre, norm_mix_post,
           w_in, attn_sinks, w_attn_branch, conv_w, conv_b, dt_bias, a_log, d_skip, ssm_norm, w_ssm_branch,
           w_out, norm_ffn_pre, norm_ffn_post, w_gate_up, w_down):
    assert x_prompt.shape == (BATCH, SEQ, D_MODEL) and x_sample.shape == (DEC_BATCH, 1, D_MODEL)
    assert w_in.shape[0] == 1, "one trunk layer"
    xp = x_prompt.reshape(NP, D_MODEL)
    xs = x_sample.reshape(NS, D_MODEL)
    w_in_t = jnp.swapaxes(w_in[0], 0, 1)
    sct = jnp.swapaxes(state_conv[0], 0, 1)

    hn_p, hn_s, dtg, dtt, dtsg = _prenorm(xp, xs, norm_mix_pre[0], w_in_t, dt_bias[0])

    first = lambda rs, rows, cols: [rs[0]]
    (proj,) = _rows_matmul("in_proj", [(hn_p, hn_s)], [(w_in_t, 0, 0, True)], [(0, 0)], first, [(bf16, f32)],
                           tn=1024, n_panels=N_MAIN // 1024)
    proj_p, proj_s = proj

    attn_p, win_k_p = _attn_prompt(proj_p, attn_sinks[0])
    cos_s, sin_s = _rope_tables(jnp.full((1,), PAST_LEN, dtype=jnp.int32))
    attn_s, win_k_s, win_v_s = _attn_sample(proj_s, cache_win_k[0], cache_win_v[0], cos_s, sin_s, attn_sinks[0])

    y_p, ssm_p = _ssd_prompt(proj_p, dtg, dtt, conv_w[0], conv_b[0], a_log[0], d_skip[0], ssm_norm[0])
    y_s, ssm_s = _ssd_sample(proj_s, dtsg, sct, state_ssm[0], conv_w[0], conv_b[0], a_log[0],
                             d_skip[0], ssm_norm[0])

    (gates,) = _rows_matmul("gates", [(hn_p, hn_s)], [(w_in_t, OFF_DT // 1024, OFF_GATES - OFF_DT, True)], [(0, 0)],
                            lambda rs, rows, cols: [_sigmoid(rs[0])], [(bf16, f32)],
                            tn=1024, n_panels=2 * D_MODEL // 1024)
    (ssm_d,) = _rows_matmul("ssm_branch", [(y_p, y_s)], [(w_ssm_branch[0], 0, 0, False)], [(0, 0)],
                            lambda rs, rows, cols: [rows[0] * rs[0]], [(bf16, f32)],
                            tn=512, n_panels=D_MODEL // 512, row_pairs=[gates + (D_MODEL // 512,)])
    (merged,) = _rows_matmul("attn_branch_merge", [(attn_p, attn_s)], [(w_attn_branch[0], 0, 0, False)], [(0, 0)],
                             lambda rs, rows, cols: [rows[0] * rs[0] + rows[1]], [(bf16, f32)],
                             tn=1024, n_panels=D_MODEL // 1024, row_pairs=[gates + (0,), ssm_d + (0,)])
    def mix_residual(rs, rows, cols):
        h = rows[0] + _rms(rs[0], cols[0])
        return [h, _rms(h, cols[1])]

    (h_p, h_s), (hn2_p, hn2_s) = _rows_matmul(
        "out_proj_residual", [merged], [(w_out[0], 0, 0, False)], [(0, 0)], mix_residual, [(f32, f32), (bf16, f32)],
        tn=D_MODEL, n_panels=1, row_pairs=[(xp, xs, 0)],
        cols=[norm_mix_post[0].reshape(1, D_MODEL), norm_ffn_pre[0].reshape(1, D_MODEL)],
        tp=256, single_buffer_weights=True)

    (act,) = _rows_matmul("ffn_gate_up", [(hn2_p, hn2_s)],
                          [(w_gate_up[0], 0, 0, False), (w_gate_up[0], D_FF // 512, 0, False)],
                          [(0, 0), (0, 1)], lambda rs, rows, cols: [_silu(rs[0]) * rs[1]], [(bf16, f32)],
                          tn=512, n_panels=D_FF // 512, tp=2 * TP)
    (ffn,) = _rows_matmul("ffn_down", [act], [(w_down[0], 0, 0, False)], [(0, 0)], first, [(bf16, f32)],
                          tn=512, n_panels=D_MODEL // 512, single_buffer_weights=True)
    out_p, out_s = _resid_norm("ffn_residual", ffn, (h_p, h_s), norm_ffn_post[0])

    proj_b = proj_p.reshape(BATCH, SEQ, N_MAIN)
    win_v_p = proj_b[:, SEQ - WINDOW:, OFF_V:OFF_Z].astype(f32)
    conv_p = proj_b[:, SEQ - (CONV_W - 1):, OFF_XBC:OFF_DT].astype(f32)
    conv_s = jnp.swapaxes(jnp.concatenate([sct[1:], proj_s[None, :, OFF_XBC:OFF_DT]], axis=0), 0, 1)
    return (out_p.reshape(BATCH, SEQ, D_MODEL),
            out_s.reshape(DEC_BATCH, 1, D_MODEL),
            win_k_p.reshape(1, BATCH, WINDOW, N_KV_HEADS, HEAD_DIM),
            win_v_p.reshape(1, BATCH, WINDOW, N_KV_HEADS, HEAD_DIM),
            conv_p.reshape(1, BATCH, CONV_W - 1, CONV_DIM),
            ssm_p.reshape(1, BATCH, N_SSM_HEADS, SSM_HEAD_DIM, D_STATE),
            win_k_s[None],
            win_v_s[None],
            conv_s.reshape(1, NS, CONV_W - 1, CONV_DIM),
            ssm_s.reshape(1, NS, N_SSM_HEADS, SSM_HEAD_DIM, D_STATE))
```

```python
import functools

import numpy as np
import jax
import jax.numpy as jnp
from jax import lax
from jax.experimental import pallas as pl
from jax.experimental.pallas import tpu as pltpu

f32 = jnp.float32
bf16 = jnp.bfloat16

D_MODEL = 2048
BATCH = 4
SEQ = 2048
DEC_BATCH = 128
PAST_LEN = 16384
N_HEADS = 32
N_KV_HEADS = 8
HEAD_DIM = 64
GQA_GROUP = 4
WINDOW = 128
ROPE_THETA = 10000.0
ATTN_WIDTH = N_HEADS * HEAD_DIM
KV_WIDTH = N_KV_HEADS * HEAD_DIM
D_INNER = 4096
SSM_HEAD_DIM = 64
N_SSM_HEADS = 64
D_STATE = 128
N_SSM_GROUPS = 8
HEADS_PER_GROUP = 8
GROUP_WIDTH = HEADS_PER_GROUP * SSM_HEAD_DIM
CONV_W = 4
CONV_DIM = D_INNER + 2 * N_SSM_GROUPS * D_STATE
CHUNK = 128
D_FF = 5632
EPS = 1e-6

OFF_Q = 0
OFF_K = ATTN_WIDTH
OFF_V = OFF_K + KV_WIDTH
OFF_Z = OFF_V + KV_WIDTH
OFF_XBC = OFF_Z + D_INNER
OFF_DT = OFF_XBC + CONV_DIM
OFF_GATES = OFF_DT + N_SSM_HEADS
N_MAIN = OFF_DT

NP = BATCH * SEQ
NS = DEC_BATCH
TP = 1024
NTP = NP // TP
TR = 512
NTR = NP // TR
N_CHUNKS = SEQ // CHUNK

LANES = 128
VMEM_LIMIT = 56 * 1024 * 1024


def _cparams(sem):
    return pltpu.CompilerParams(dimension_semantics=sem, vmem_limit_bytes=VMEM_LIMIT)


def _split2(v):
    hi = v.astype(bf16)
    lo = (v - hi.astype(f32)).astype(bf16)
    return hi, lo


def _split3(v):
    hi = v.astype(bf16)
    r = v - hi.astype(f32)
    mid = r.astype(bf16)
    lo = (r - mid.astype(f32)).astype(bf16)
    return hi, mid, lo


def _dot(a, b):
    return jnp.dot(a, b, preferred_element_type=f32)


def _dot_nt(a, b):
    return lax.dot_general(a, b, (((1,), (1,)), ((), ())), preferred_element_type=f32)


def _dot_tn(a, b):
    return lax.dot_general(a, b, (((0,), (0,)), ((), ())), preferred_element_type=f32)


def _dot_f32_lhs(a, b_exact):
    p = _split3(a)
    return _dot(p[0], b_exact) + _dot(p[1], b_exact) + _dot(p[2], b_exact)


def _dot_f32_rhs(a_exact, b):
    p = _split3(b)
    return _dot(a_exact, p[0]) + _dot(a_exact, p[1]) + _dot(a_exact, p[2])


def _sigmoid(x):
    return 0.5 * jnp.tanh(0.5 * x) + 0.5


def _silu(x):
    return x * _sigmoid(x)


def _softplus(x):
    return jnp.maximum(x, 0.0) + jnp.log1p(jnp.exp(-jnp.abs(x)))


def _rms(x, w):
    return x * lax.rsqrt(jnp.mean(x * x, axis=-1, keepdims=True) + EPS) * w


def _head_selector(n_heads, width):
    r = lax.broadcasted_iota(jnp.int32, (n_heads, n_heads * width), 0)
    c = lax.broadcasted_iota(jnp.int32, (n_heads, n_heads * width), 1)
    return jnp.where(c // width == r, 1.0, 0.0).astype(bf16)


def _prenorm_body(xp_ref, xs_ref, nw_ref, wdt_ref, dtb_ref, hnp_ref, hns_ref, dtg_ref, dtt_ref, dtsg_ref):
    i = pl.program_id(0)
    w_pieces = jnp.concatenate(_split2(wdt_ref[0:N_SSM_HEADS, :]), axis=0)

    def dt_rows(hn, grouped_ref):
        both = _dot_nt(w_pieces, hn.astype(bf16))
        raw_t = both[0:N_SSM_HEADS] + both[N_SSM_HEADS:2 * N_SSM_HEADS]
        dt_t = _softplus(raw_t + dtb_ref[...])
        dt = dt_t.T
        for g in range(N_SSM_GROUPS):
            grouped_ref[g] = dt[:, g * HEADS_PER_GROUP:(g + 1) * HEADS_PER_GROUP]
        return dt_t

    @pl.when(i < NTP)
    def _():
        hn = _rms(xp_ref[...], nw_ref[...])
        hnp_ref[...] = hn.astype(bf16)
        dtt_ref[...] = dt_rows(hn, dtg_ref)

    @pl.when(i == NTP)
    def _():
        hn = _rms(xs_ref[...], nw_ref[...])
        hns_ref[...] = hn
        dt_rows(hn, dtsg_ref)


def _prenorm(xp, xs, norm_w, w_in_t, dt_bias):
    pidx = lambda i: (jnp.minimum(i, NTP - 1), 0)
    const = lambda i: (0, 0)
    return pl.pallas_call(
        _prenorm_body,
        grid=(NTP + 1,),
        in_specs=[
            pl.BlockSpec((TP, D_MODEL), pidx),
            pl.BlockSpec((NS, D_MODEL), const),
            pl.BlockSpec((1, D_MODEL), const),
            pl.BlockSpec((LANES, D_MODEL), lambda i: (OFF_DT // LANES, 0)),
            pl.BlockSpec((N_SSM_HEADS, 1), const),
        ],
        out_specs=[
            pl.BlockSpec((TP, D_MODEL), pidx),
            pl.BlockSpec((NS, D_MODEL), const),
            pl.BlockSpec((N_SSM_GROUPS, TP, HEADS_PER_GROUP), lambda i: (0, jnp.minimum(i, NTP - 1), 0)),
            pl.BlockSpec((N_SSM_HEADS, TP), lambda i: (0, jnp.minimum(i, NTP - 1))),
            pl.BlockSpec((N_SSM_GROUPS, NS, HEADS_PER_GROUP), lambda i: (0, 0, 0)),
        ],
        out_shape=[
            jax.ShapeDtypeStruct((NP, D_MODEL), bf16),
            jax.ShapeDtypeStruct((NS, D_MODEL), f32),
            jax.ShapeDtypeStruct((N_SSM_GROUPS, NP, HEADS_PER_GROUP), f32),
            jax.ShapeDtypeStruct((N_SSM_HEADS, NP), f32),
            jax.ShapeDtypeStruct((N_SSM_GROUPS, NS, HEADS_PER_GROUP), f32),
        ],
        compiler_params=_cparams(("arbitrary",)),
        name="prenorm_dt",
    )(xp, xs, norm_w.reshape(1, D_MODEL), w_in_t, dt_bias.reshape(-1, 1))


CAST_CHUNK = 256


def _rows_matmul_body(*refs, n_a, n_w, n_row, n_col, n_out, dots, epilogue, shifts, transposed):
    pos = 0
    a_p = refs[pos:pos + n_a]; pos += n_a
    a_s = refs[pos:pos + n_a]; pos += n_a
    w = refs[pos:pos + n_w]; pos += n_w
    n_tail = sum(1 for s in shifts if s)
    tails = iter(refs[pos:pos + n_tail]); pos += n_tail
    w_tail = [next(tails) if s else None for s in shifts]
    row_p = refs[pos:pos + n_row]; pos += n_row
    row_s = refs[pos:pos + n_row]; pos += n_row
    col = refs[pos:pos + n_col]; pos += n_col
    out_p = refs[pos:pos + n_out]; pos += n_out
    out_s = refs[pos:pos + n_out]; pos += n_out
    wb = refs[pos:pos + n_w]
    i = pl.program_id(1)

    @pl.when(i == 0)
    def _():
        for k in range(n_w):
            if transposed[k]:
                tn = w[k].shape[0]
                for c in range(tn // CAST_CHUNK):
                    r0 = c * CAST_CHUNK + shifts[k]
                    if r0 + CAST_CHUNK <= tn:
                        blk = w[k][r0:r0 + CAST_CHUNK, :]
                    else:
                        blk = jnp.concatenate([w[k][r0:tn, :], w_tail[k][0:r0 + CAST_CHUNK - tn, :]], axis=0)
                    wb[k][:, c * CAST_CHUNK:(c + 1) * CAST_CHUNK] = blk.T.astype(bf16)
                continue

            def cast(c, carry, k=k):
                r0 = pl.multiple_of(c * CAST_CHUNK, CAST_CHUNK)
                blk = w[k][pl.ds(r0, CAST_CHUNK), :]
                if shifts[k]:
                    blk = jnp.concatenate([blk[:, shifts[k]:], w_tail[k][pl.ds(r0, CAST_CHUNK), 0:shifts[k]]], axis=1)
                wb[k][pl.ds(r0, CAST_CHUNK), :] = blk.astype(bf16)
                return carry

            lax.fori_loop(0, w[k].shape[0] // CAST_CHUNK, cast, 0)

    def compute(a, row, out):
        rs = [_dot(a[ai][...].astype(bf16), wb[wi][...]) for ai, wi in dots]
        res = epilogue(rs, [r[...].astype(f32) for r in row], [c[...] for c in col])
        for o, v in zip(out, res):
            o[...] = v.astype(o.dtype)

    @pl.when(i == 0)
    def _():
        compute(a_s, row_s, out_s)

    @pl.when(i > 0)
    def _():
        compute(a_p, row_p, out_p)


def _rows_matmul(name, a_pairs, weights, dots, epilogue, out_dtypes, tn, n_panels,
                 row_pairs=(), cols=(), tp=TP, single_buffer_weights=False):
    n_a, n_w, n_row, n_col, n_out = len(a_pairs), len(weights), len(row_pairs), len(cols), len(out_dtypes)
    ptile = lambda i: jnp.maximum(i - 1, 0)
    pidx = lambda n, i: (ptile(i), 0)
    sidx = lambda n, i: (0, 0)
    in_specs, args = [], []
    for ap, _ in a_pairs:
        in_specs.append(pl.BlockSpec((tp, ap.shape[1]), pidx)); args.append(ap)
    for _, as_ in a_pairs:
        in_specs.append(pl.BlockSpec((NS, as_.shape[1]), sidx)); args.append(as_)
    w_mode = dict(pipeline_mode=pl.Buffered(1)) if single_buffer_weights else {}
    kdim = lambda wk, tr: wk.shape[1] if tr else wk.shape[0]
    for wk, off, _, tr in weights:
        panel = functools.partial(lambda n, i, off: (0, n + off), off=off)
        spec = (pl.BlockSpec((tn, kdim(wk, tr)), lambda n, i, panel=panel: panel(n, i)[::-1], **w_mode) if tr
                else pl.BlockSpec((kdim(wk, tr), tn), panel, **w_mode))
        in_specs.append(spec); args.append(wk)
    for wk, off, shift, tr in weights:
        if shift:
            after = functools.partial(lambda n, i, off: (0, (n + off + 1) * (tn // LANES)), off=off)
            spec = (pl.BlockSpec((LANES, kdim(wk, tr)), lambda n, i, after=after: after(n, i)[::-1], **w_mode) if tr
                    else pl.BlockSpec((kdim(wk, tr), LANES), after, **w_mode))
            in_specs.append(spec); args.append(wk)
    out_p_idx = lambda n, i: (ptile(i), n)
    out_s_idx = lambda n, i: (0, n)
    for rp, _, off in row_pairs:
        in_specs.append(pl.BlockSpec((tp, tn), functools.partial(lambda n, i, off: (ptile(i), n + off), off=off)))
        args.append(rp)
    for _, rs, off in row_pairs:
        in_specs.append(pl.BlockSpec((NS, tn), functools.partial(lambda n, i, off: (0, n + off), off=off)))
        args.append(rs)
    for c in cols:
        in_specs.append(pl.BlockSpec((1, tn), lambda n, i: (0, n))); args.append(c)
    width = n_panels * tn
    out_specs = ([pl.BlockSpec((tp, tn), out_p_idx)] * n_out + [pl.BlockSpec((NS, tn), out_s_idx)] * n_out)
    out_shape = ([jax.ShapeDtypeStruct((NP, width), dp) for dp, _ in out_dtypes]
                 + [jax.ShapeDtypeStruct((NS, width), ds) for _, ds in out_dtypes])
    scratch = [pltpu.VMEM((kdim(wk, tr), tn), bf16) for wk, _, _, tr in weights]
    body = functools.partial(_rows_matmul_body, n_a=n_a, n_w=n_w, n_row=n_row, n_col=n_col, n_out=n_out,
                             dots=tuple(dots), epilogue=epilogue,
                             shifts=tuple(s for _, _, s, _ in weights),
                             transposed=tuple(tr for _, _, _, tr in weights))
    res = pl.pallas_call(
        body,
        grid=(n_panels, NP // tp + 1),
        in_specs=in_specs,
        out_specs=out_specs,
        out_shape=out_shape,
        scratch_shapes=scratch,
        compiler_params=_cparams(("arbitrary", "arbitrary")),
        name=name,
    )(*args)
    return [(res[k], res[n_out + k]) for k in range(n_out)]


def _rope_tables(pos):
    half = HEAD_DIM // 2
    inv = ROPE_THETA ** (-jnp.arange(half, dtype=f32) / half)
    ang = pos.astype(f32)[:, None] * inv[None, :]
    cos, sin = jnp.cos(ang), jnp.sin(ang)
    return jnp.concatenate([cos, cos, cos, cos], axis=1), jnp.concatenate([-sin, sin, -sin, sin], axis=1)


def _rope_chunk(x, cos, sin, first_half):
    from_right = pltpu.roll(x, 96, 1)
    from_left = pltpu.roll(x, 32, 1)
    return x * cos + jnp.where(first_half, from_right, from_left) * sin


def _rope(x, cos, sin):
    rows, width = x.shape
    lane = lax.broadcasted_iota(jnp.int32, (rows, LANES), 1)
    first_half = (lane % HEAD_DIM) < (HEAD_DIM // 2)
    return [_rope_chunk(x[:, c * LANES:(c + 1) * LANES], cos, sin, first_half) for c in range(width // LANES)]


def _attn_prompt_body(sink_ref, q_ref, kv_ref, cos_ref, sin_ref, cost_ref, sint_ref, bias_ref,
                      o_ref, wk_ref, kbuf, vbuf_t):
    n = pl.program_id(1)
    cur = n % 2
    prev = 1 - cur

    @pl.when(n == 0)
    def _():
        kbuf[1] = jnp.zeros((WINDOW, KV_WIDTH), bf16)
        vbuf_t[1] = jnp.zeros((KV_WIDTH, WINDOW), bf16)

    kc = _rope(kv_ref[:, 0:KV_WIDTH].astype(f32), cos_ref[...], sin_ref[...])
    for c, v in enumerate(kc):
        kbuf[cur, :, c * LANES:(c + 1) * LANES] = v.astype(bf16)
    vbuf_t[cur] = kv_ref[:, KV_WIDTH:2 * KV_WIDTH].astype(f32).T.astype(bf16)

    @pl.when(n == N_CHUNKS - 1)
    def _():
        for c, v in enumerate(kc):
            wk_ref[:, c * LANES:(c + 1) * LANES] = v

    qt = q_ref[...].astype(f32).T
    cost, sint = cost_ref[...], sint_ref[...]
    bias = bias_ref[...]
    scale = HEAD_DIM ** -0.5
    half = HEAD_DIM // 2
    pad = jnp.zeros((HEAD_DIM, CHUNK), f32)
    heads_per_pair = 2 * GQA_GROUP
    heads = []
    for pr in range(N_KV_HEADS // 2):
        qcols = []
        for hh in range(heads_per_pair):
            h = pr * heads_per_pair + hh
            x1 = qt[h * HEAD_DIM:h * HEAD_DIM + half, :]
            x2 = qt[h * HEAD_DIM + half:(h + 1) * HEAD_DIM, :]
            r1 = (x1 * cost - x2 * sint) * scale
            r2 = (x2 * cost + x1 * sint) * scale
            qcols.append(jnp.concatenate([r1, r2, pad] if hh < GQA_GROUP else [pad, r1, r2], axis=0))
        qw = jnp.concatenate(qcols, axis=1).astype(bf16)
        lanes = slice(pr * LANES, (pr + 1) * LANES)
        kpair = jnp.concatenate([kbuf[prev, :, lanes], kbuf[cur, :, lanes]], axis=0)
        st = _dot(kpair, qw)
        st = st + jnp.concatenate([bias] * heads_per_pair, axis=1)
        sink = sink_ref[:, pr * heads_per_pair * CHUNK:(pr + 1) * heads_per_pair * CHUNK]
        m = jnp.maximum(jnp.max(st, axis=0, keepdims=True), sink)
        p = jnp.exp(st - m)
        inv = 1.0 / (jnp.sum(p, axis=0, keepdims=True) + jnp.exp(sink - m))
        pb = p.astype(bf16)
        for k in range(2):
            g = 2 * pr + k
            cols = slice(k * GQA_GROUP * CHUNK, (k + 1) * GQA_GROUP * CHUNK)
            dims = slice(g * HEAD_DIM, (g + 1) * HEAD_DIM)
            vgt = jnp.concatenate([vbuf_t[prev, dims, :], vbuf_t[cur, dims, :]], axis=1)
            og = _dot(vgt, pb[:, cols]) * inv[:, cols]
            heads += [og[:, j * CHUNK:(j + 1) * CHUNK] for j in range(GQA_GROUP)]
    o_ref[...] = jnp.concatenate(heads, axis=0).T.astype(o_ref.dtype)


def _band_bias():
    qi = np.arange(WINDOW)[None, :]
    kj = np.arange(2 * WINDOW)[:, None]
    cur = (kj >= WINDOW) & (kj - WINDOW <= qi)
    prev = (kj < WINDOW) & (kj > qi)
    neg = np.float32(-np.inf)
    b0 = np.where(cur, np.float32(0), neg)
    b1 = np.where(cur | prev, np.float32(0), neg)
    return jnp.asarray(np.stack([b0, b1]).astype(np.float32))


def _attn_prompt(proj_p, sinks):
    pos = jnp.arange(SEQ, dtype=jnp.int32)
    cos, sin = _rope_tables(pos)
    half = HEAD_DIM // 2
    cos_t, sin_t = cos[:, :half].T, sin[:, half:HEAD_DIM].T
    row = lambda b, n: b * N_CHUNKS + n
    return pl.pallas_call(
        _attn_prompt_body,
        grid=(BATCH, N_CHUNKS),
        in_specs=[
            pl.BlockSpec((1, N_HEADS * CHUNK), lambda b, n: (0, 0)),
            pl.BlockSpec((CHUNK, ATTN_WIDTH), lambda b, n: (row(b, n), 0)),
            pl.BlockSpec((CHUNK, 2 * KV_WIDTH), lambda b, n: (row(b, n), OFF_K // (2 * KV_WIDTH))),
            pl.BlockSpec((CHUNK, LANES), lambda b, n: (n, 0)),
            pl.BlockSpec((CHUNK, LANES), lambda b, n: (n, 0)),
            pl.BlockSpec((half, CHUNK), lambda b, n: (0, n)),
            pl.BlockSpec((half, CHUNK), lambda b, n: (0, n)),
            pl.BlockSpec((None, 2 * WINDOW, CHUNK), lambda b, n: (jnp.minimum(n, 1), 0, 0)),
        ],
        out_specs=[
            pl.BlockSpec((CHUNK, ATTN_WIDTH), lambda b, n: (row(b, n), 0)),
            pl.BlockSpec((None, WINDOW, KV_WIDTH), lambda b, n: (b, 0, 0)),
        ],
        out_shape=[
            jax.ShapeDtypeStruct((NP, ATTN_WIDTH), bf16),
            jax.ShapeDtypeStruct((BATCH, WINDOW, KV_WIDTH), f32),
        ],
        scratch_shapes=[pltpu.VMEM((2, WINDOW, KV_WIDTH), bf16), pltpu.VMEM((2, KV_WIDTH, WINDOW), bf16)],
        compiler_params=_cparams(("arbitrary", "arbitrary")),
        name="attn_prompt",
    )(jnp.repeat(sinks, CHUNK).reshape(1, N_HEADS * CHUNK), proj_p, proj_p, cos, sin, cos_t, sin_t, _band_bias())


SB = 8


def _attn_sample_body(q_ref, k_ref, v_ref, kt_ref, vt_ref, cos_ref, sin_ref, sink_ref, o_ref, kto_ref, vto_ref):
    cos, sin = cos_ref[...], sin_ref[...]
    scale = HEAD_DIM ** -0.5
    cw = kt_ref.shape[2]
    is_new = lax.broadcasted_iota(jnp.int32, (KV_WIDTH, cw), 1) == cw - 1
    head = lax.broadcasted_iota(jnp.int32, (N_HEADS, KV_WIDTH), 0)
    col = lax.broadcasted_iota(jnp.int32, (N_HEADS, KV_WIDTH), 1)
    own = (col // HEAD_DIM) == (head // GQA_GROUP)
    sink = sink_ref[...]
    piece_row = lax.broadcasted_iota(jnp.int32, (3 * SB, SB * cw), 0) % SB
    lane = lax.broadcasted_iota(jnp.int32, (3 * SB, SB * cw), 1)
    place = jnp.where(lane == piece_row * cw + cw - 1, 1.0, 0.0).astype(bf16)

    def as_last_columns(rows):
        return _dot_tn(jnp.concatenate(_split3(rows), axis=0), place)

    kn_cols = as_last_columns(jnp.concatenate(_rope(k_ref[...], cos, sin), axis=1))
    vn_cols = as_last_columns(v_ref[...])
    for b in range(SB):
        def appended(t_ref, new_cols):
            return jnp.where(is_new, new_cols[:, b * cw:(b + 1) * cw], pltpu.roll(t_ref[b], cw - 1, 1))
        ktn = appended(kt_ref, kn_cols)
        vtn = appended(vt_ref, vn_cols)
        kto_ref[b] = ktn
        vto_ref[b] = vtn
        qr = jnp.concatenate(_rope(q_ref[b], cos, sin), axis=1) * scale
        s = _dot(jnp.where(own, qr, 0.0).astype(bf16), ktn.astype(bf16))
        m = jnp.maximum(jnp.max(s, axis=1, keepdims=True), sink)
        p = jnp.exp(s - m)
        inv = 1.0 / (jnp.sum(p, axis=1, keepdims=True) + jnp.exp(sink - m))
        o = jnp.where(own, _dot_nt((p * inv).astype(bf16), vtn.astype(bf16)), 0.0)
        o8 = o[0:8] + o[8:16] + o[16:24] + o[24:32]
        o_ref[b] = (o8 + pltpu.roll(o8, GQA_GROUP, 0))[0:GQA_GROUP]


def _attn_sample(proj_s, cache_k, cache_v, cos, sin, sinks):
    cw = cache_k.shape[1]
    assert cw == WINDOW, "the new token's window is exactly the cache minus its oldest entry"
    as_t = lambda c: jnp.transpose(c, (0, 2, 3, 1)).reshape(NS, KV_WIDTH, cw)
    q_rep = jnp.tile(proj_s[:, OFF_Q:OFF_K].reshape(NS, N_HEADS, HEAD_DIM), (1, 1, N_KV_HEADS))
    rows = lambda i: (i, 0)
    rows3 = lambda i: (i, 0, 0)
    const = lambda i: (0, 0)
    o, kto, vto = pl.pallas_call(
        _attn_sample_body,
        grid=(NS // SB,),
        in_specs=[
            pl.BlockSpec((SB, N_HEADS, KV_WIDTH), rows3),
            pl.BlockSpec((SB, KV_WIDTH), lambda i: (i, OFF_K // KV_WIDTH)),
            pl.BlockSpec((SB, KV_WIDTH), lambda i: (i, OFF_V // KV_WIDTH)),
            pl.BlockSpec((SB, KV_WIDTH, cw), rows3),
            pl.BlockSpec((SB, KV_WIDTH, cw), rows3),
            pl.BlockSpec((1, LANES), const),
            pl.BlockSpec((1, LANES), const),
            pl.BlockSpec((N_HEADS, 1), const),
        ],
        out_specs=[
            pl.BlockSpec((SB, GQA_GROUP, KV_WIDTH), rows3),
            pl.BlockSpec((SB, KV_WIDTH, cw), rows3),
            pl.BlockSpec((SB, KV_WIDTH, cw), rows3),
        ],
        out_shape=[
            jax.ShapeDtypeStruct((NS, GQA_GROUP, KV_WIDTH), f32),
            jax.ShapeDtypeStruct((NS, KV_WIDTH, cw), f32),
            jax.ShapeDtypeStruct((NS, KV_WIDTH, cw), f32),
        ],
        compiler_params=_cparams(("arbitrary",)),
        name="attn_sample",
    )(q_rep, proj_s, proj_s, as_t(cache_k), as_t(cache_v), cos, sin, sinks.reshape(N_HEADS, 1))
    attn = o.reshape(NS, GQA_GROUP, N_KV_HEADS, HEAD_DIM).transpose(0, 2, 1, 3).reshape(NS, ATTN_WIDTH)
    back = lambda t: jnp.transpose(t.reshape(NS, N_KV_HEADS, HEAD_DIM, cw), (0, 3, 1, 2))
    return attn, back(kto), back(vto)


GPS = 2
XW = GPS * GROUP_WIDTH
BW = GPS * D_STATE
XBCW = XW + 2 * BW
CPS = 8
TAIL = 16


def _ssd_prompt_body(x_ref, b_ref, c_ref, z_ref, dt_ref, dtt_ref, cwx_ref, cwb_ref, cwc_ref,
                     cbx_ref, cbb_ref, cbc_ref, alogt_ref, dsk_ref, nw_ref,
                     y_ref, st_ref, hst, tails):
    c = pl.program_id(2)
    cur = c % 2
    prev = 1 - cur

    @pl.when(c == 0)
    def _():
        hst[1] = jnp.zeros((GPS, D_STATE, GROUP_WIDTH), f32)
        tails[1] = jnp.zeros((TAIL, XBCW), bf16)

    ti = lax.broadcasted_iota(jnp.int32, (CONV_W * CHUNK, 2 * CHUNK), 0)
    si = lax.broadcasted_iota(jnp.int32, (CONV_W * CHUNK, 2 * CHUNK), 1)
    pick = jnp.where(si == CHUNK + (ti % CHUNK) - (ti // CHUNK), 1.0, 0.0).astype(bf16)
    w = jnp.concatenate([cwx_ref[...], cwb_ref[...], cwc_ref[...]], axis=1)
    bias = jnp.concatenate([cbx_ref[...], cbb_ref[...], cbc_ref[...]], axis=1)
    li = lax.broadcasted_iota(jnp.int32, (CHUNK, CHUNK), 0)
    si = lax.broadcasted_iota(jnp.int32, (CHUNK, CHUNK), 1)
    causal = li >= si
    trit = jnp.where(li <= si, 1.0, 0.0).astype(bf16)
    sel = _head_selector(HEADS_PER_GROUP, SSM_HEAD_DIM)

    def expand(v):
        hi = v.astype(bf16)
        lo = (v - hi.astype(f32)).astype(bf16)
        return _dot(hi, sel) + _dot(lo, sel)

    expand_row = lambda v: _dot_f32_lhs(jnp.broadcast_to(v, (8, HEADS_PER_GROUP)), sel)[0:1, :]
    lane = lax.broadcasted_iota(jnp.int32, (CHUNK, LANES), 1)
    zero = jnp.zeros((CHUNK, LANES), bf16)
    log2e = 1.4426950408889634

    tail = tails[prev]
    h = [hst[prev, k] for k in range(GPS)]
    for cc in range(CPS):
        rows = slice(cc * CHUNK, (cc + 1) * CHUNK)
        xbc = jnp.concatenate([x_ref[rows, :], b_ref[rows, :], c_ref[rows, :]], axis=1)
        stacked = jnp.concatenate([jnp.zeros((CHUNK - TAIL, XBCW), bf16), tail, xbc], axis=0)
        tail = xbc[CHUNK - TAIL:, :]
        taps = _dot(pick, stacked)
        acc = bias
        for j in range(CONV_W):
            acc = acc + w[CONV_W - 1 - j:CONV_W - j, :] * taps[j * CHUNK:(j + 1) * CHUNK, :]
        act = _silu(acc)

        for k in range(GPS):
            xs = act[:, k * GROUP_WIDTH:(k + 1) * GROUP_WIDTH]
            bmb = act[:, XW + k * D_STATE:XW + (k + 1) * D_STATE].astype(bf16)
            cmb = act[:, XW + BW + k * D_STATE:XW + BW + (k + 1) * D_STATE].astype(bf16)
            dtt = dtt_ref[k * HEADS_PER_GROUP:(k + 1) * HEADS_PER_GROUP, rows]
            dat = dtt * (-jnp.exp(alogt_ref[k]))
            acumt = _dot_f32_lhs(dat, trit)
            acum = acumt.T
            a_last = acum[CHUNK - 1:CHUNK, :]
            acum2 = acum * log2e
            src2 = acumt * log2e - jnp.log2(dtt)

            xb = xs.astype(bf16)
            xdec = (xs * expand(dt_ref[k, rows, :] * jnp.exp(a_last - acum))).astype(bf16)

            cb = _dot_nt(cmb, bmb)
            pairs = []
            for pr in range(HEADS_PER_GROUP // 2):
                ms = []
                for r in (2 * pr, 2 * pr + 1):
                    seg = acum2[:, r:r + 1] - src2[r:r + 1, :]
                    ms.append((cb * jnp.exp2(jnp.where(causal, seg, -jnp.inf))).astype(bf16))
                xp = xb[:, pr * LANES:(pr + 1) * LANES]
                x0 = jnp.where(lane < SSM_HEAD_DIM, xp, zero)
                x1 = jnp.where(lane < SSM_HEAD_DIM, zero, xp)
                pairs.append(_dot(jnp.concatenate(ms, axis=1), jnp.concatenate([x0, x1], axis=0)))
            y = jnp.concatenate(pairs, axis=1)

            h_prev = h[k]
            y = y + _dot(cmb, h_prev.astype(bf16)) * expand(jnp.exp2(acum2))
            y = y + expand_row(dsk_ref[k]) * xs

            cols = slice(k * GROUP_WIDTH, (k + 1) * GROUP_WIDTH)
            gz = y * _silu(z_ref[rows, cols].astype(f32))
            y_ref[rows, cols] = _rms(gz, nw_ref[:, cols]).astype(y_ref.dtype)

            h_new = h_prev * expand_row(jnp.exp(a_last)) + _dot_tn(bmb, xdec)
            h[k] = h_new

    tails[cur] = tail
    for k in range(GPS):
        hst[cur, k] = h[k]

    @pl.when(c == N_CHUNKS // CPS - 1)
    def _():
        for k in range(GPS):
            st_ref[k] = hst[cur, k].T


def _ssd_prompt(proj_p, dtg, dtt, conv_w, conv_b, a_log, d_skip, ssm_norm):
    steps = N_CHUNKS // CPS
    rows = CPS * CHUNK
    row = lambda b, g, c: b * steps + c
    xoff = OFF_XBC // XW
    boff = (OFF_XBC + D_INNER) // BW
    coff = boff + N_SSM_GROUPS // GPS
    zoff = OFF_Z // XW
    wboff = D_INNER // BW
    wcoff = wboff + N_SSM_GROUPS // GPS
    alog_t = a_log.reshape(N_SSM_GROUPS, HEADS_PER_GROUP, 1)
    dsk_g = d_skip.reshape(N_SSM_GROUPS, 1, HEADS_PER_GROUP)
    conv_b = conv_b.reshape(1, CONV_DIM)
    grp = lambda b, g, c: (g, 0, 0)
    return pl.pallas_call(
        _ssd_prompt_body,
        grid=(BATCH, N_SSM_GROUPS // GPS, steps),
        in_specs=[
            pl.BlockSpec((rows, XW), lambda b, g, c: (row(b, g, c), xoff + g)),
            pl.BlockSpec((rows, BW), lambda b, g, c: (row(b, g, c), boff + g)),
            pl.BlockSpec((rows, BW), lambda b, g, c: (row(b, g, c), coff + g)),
            pl.BlockSpec((rows, XW), lambda b, g, c: (row(b, g, c), zoff + g)),
            pl.BlockSpec((GPS, rows, HEADS_PER_GROUP), lambda b, g, c: (g, row(b, g, c), 0)),
            pl.BlockSpec((GPS * HEADS_PER_GROUP, rows), lambda b, g, c: (g, row(b, g, c))),
            pl.BlockSpec((CONV_W, XW), lambda b, g, c: (0, g)),
            pl.BlockSpec((CONV_W, BW), lambda b, g, c: (0, wboff + g)),
            pl.BlockSpec((CONV_W, BW), lambda b, g, c: (0, wcoff + g)),
            pl.BlockSpec((1, XW), lambda b, g, c: (0, g)),
            pl.BlockSpec((1, BW), lambda b, g, c: (0, wboff + g)),
            pl.BlockSpec((1, BW), lambda b, g, c: (0, wcoff + g)),
            pl.BlockSpec((GPS, HEADS_PER_GROUP, 1), grp),
            pl.BlockSpec((GPS, 1, HEADS_PER_GROUP), grp),
            pl.BlockSpec((1, XW), lambda b, g, c: (0, g)),
        ],
        out_specs=[
            pl.BlockSpec((rows, XW), lambda b, g, c: (row(b, g, c), g)),
            pl.BlockSpec((None, GPS, GROUP_WIDTH, D_STATE), lambda b, g, c: (b, g, 0, 0)),
        ],
        out_shape=[
            jax.ShapeDtypeStruct((NP, D_INNER), bf16),
            jax.ShapeDtypeStruct((BATCH, N_SSM_GROUPS, GROUP_WIDTH, D_STATE), f32),
        ],
        scratch_shapes=[
            pltpu.VMEM((2, GPS, D_STATE, GROUP_WIDTH), f32),
            pltpu.VMEM((2, TAIL, XBCW), bf16),
        ],
        compiler_params=_cparams(("arbitrary", "arbitrary", "arbitrary")),
        name="ssd_prompt",
    )(proj_p, proj_p, proj_p, proj_p, dtg, dtt, conv_w, conv_w, conv_w, conv_b, conv_b, conv_b,
      alog_t, dsk_g, ssm_norm.reshape(1, D_INNER))


SQ = 16
GSS = 2


def _ssd_sample_body(x_ref, b_ref, c_ref, sx_ref, sb_ref, sc_ref, z_ref, dt_ref, h_ref,
                     cwx_ref, cwb_ref, cwc_ref, cbx_ref, cbb_ref, cbc_ref, alog_ref, dsk_ref, nw_ref,
                     y_ref, ho_ref):
    def conv(new_ref, st_ref, w_ref, bias_ref):
        w = w_ref[...]
        acc = bias_ref[...] + w[CONV_W - 1:CONV_W, :] * new_ref[...]
        for j in range(CONV_W - 1):
            acc = acc + w[j:j + 1, :] * st_ref[j]
        return _silu(acc)

    xs_all = conv(x_ref, sx_ref, cwx_ref, cbx_ref)
    bm_all = conv(b_ref, sb_ref, cwb_ref, cbb_ref)
    cm_all = conv(c_ref, sc_ref, cwc_ref, cbc_ref)
    zs_all = _silu(z_ref[...])
    sel = _head_selector(HEADS_PER_GROUP, SSM_HEAD_DIM)
    exact = lambda v: v.astype(bf16).astype(f32)
    zeros = jnp.zeros((SQ, D_STATE), f32)
    ones = jnp.ones((SQ, D_STATE), f32)
    wide = (6 * SQ, SQ * 2 * D_STATE)
    row_seq = lax.broadcasted_iota(jnp.int32, wide, 0) % SQ
    col_seq = lax.broadcasted_iota(jnp.int32, wide, 1) // (2 * D_STATE)
    c_row = lax.broadcasted_iota(jnp.int32, (SQ, SQ * D_STATE), 0)
    c_col = lax.broadcasted_iota(jnp.int32, (SQ, SQ * D_STATE), 1) // D_STATE

    for k in range(GSS):
        xs = xs_all[:, k * GROUP_WIDTH:(k + 1) * GROUP_WIDTH]
        bm = bm_all[:, k * D_STATE:(k + 1) * D_STATE]
        cm = cm_all[:, k * D_STATE:(k + 1) * D_STATE]
        dt = dt_ref[k]
        decay = jnp.exp(dt * (-jnp.exp(alog_ref[k])))
        xdt = xs * _dot_f32_lhs(dt, sel)
        dec = _dot_f32_lhs(decay, sel)

        x_hi = exact(xdt)
        d_hi = exact(dec)
        d_mid = exact(dec - d_hi)
        b_hi = exact(bm)
        lhs = jnp.concatenate([x_hi, x_hi, xdt - x_hi, d_hi, d_mid, dec - d_hi - d_mid], axis=0)
        lhs_t = lhs.T.astype(bf16)
        outer_rows = [jnp.concatenate([v, zeros], axis=1) for v in (b_hi, bm - b_hi, b_hi)]
        rhs = jnp.concatenate(outer_rows + [jnp.concatenate([zeros, ones], axis=1)] * 3, axis=0)
        rhs_wide = jnp.where(row_seq == col_seq, jnp.concatenate([rhs] * SQ, axis=1), 0.0).astype(bf16)
        both = _dot(lhs_t, rhs_wide)
        h_bf = []
        rows = slice(k * GROUP_WIDTH, (k + 1) * GROUP_WIDTH)
        for s in range(SQ):
            upd = both[:, s * 2 * D_STATE:(s + 1) * 2 * D_STATE]
            h_new = h_ref[s, rows, :] * upd[:, D_STATE:] + upd[:, 0:D_STATE]
            ho_ref[s, rows, :] = h_new
            h_bf.append(h_new.astype(bf16))
        c_diag = jnp.where(c_row == c_col, jnp.concatenate([cm] * SQ, axis=1), 0.0).astype(bf16)
        y = _dot_nt(c_diag, jnp.concatenate(h_bf, axis=1))
        dsk = _dot_f32_lhs(jnp.broadcast_to(dsk_ref[k], (SQ, HEADS_PER_GROUP)), sel)
        gz = (y + dsk * xs) * zs_all[:, rows]
        y_ref[:, rows] = _rms(gz, nw_ref[:, rows])


def _ssd_sample(proj_s, dtsg, state_conv, state_ssm, conv_w, conv_b, a_log, d_skip, ssm_norm):
    xw, bw = GSS * GROUP_WIDTH, GSS * D_STATE
    xoff = OFF_XBC // xw
    boff = (OFF_XBC + D_INNER) // bw
    coff = boff + N_SSM_GROUPS // GSS
    zoff = OFF_Z // xw
    wboff = D_INNER // bw
    wcoff = wboff + N_SSM_GROUPS // GSS
    conv_b = conv_b.reshape(1, CONV_DIM)
    grp = lambda i, g: (g, 0, 0)
    return pl.pallas_call(
        _ssd_sample_body,
        grid=(NS // SQ, N_SSM_GROUPS // GSS),
        in_specs=[
            pl.BlockSpec((SQ, xw), lambda i, g: (i, xoff + g)),
            pl.BlockSpec((SQ, bw), lambda i, g: (i, boff + g)),
            pl.BlockSpec((SQ, bw), lambda i, g: (i, coff + g)),
            pl.BlockSpec((CONV_W - 1, SQ, xw), lambda i, g: (0, i, g)),
            pl.BlockSpec((CONV_W - 1, SQ, bw), lambda i, g: (0, i, wboff + g)),
            pl.BlockSpec((CONV_W - 1, SQ, bw), lambda i, g: (0, i, wcoff + g)),
            pl.BlockSpec((SQ, xw), lambda i, g: (i, zoff + g)),
            pl.BlockSpec((GSS, SQ, HEADS_PER_GROUP), lambda i, g: (g, i, 0)),
            pl.BlockSpec((SQ, xw, D_STATE), lambda i, g: (i, g, 0)),
            pl.BlockSpec((CONV_W, xw), lambda i, g: (0, g)),
            pl.BlockSpec((CONV_W, bw), lambda i, g: (0, wboff + g)),
            pl.BlockSpec((CONV_W, bw), lambda i, g: (0, wcoff + g)),
            pl.BlockSpec((1, xw), lambda i, g: (0, g)),
            pl.BlockSpec((1, bw), lambda i, g: (0, wboff + g)),
            pl.BlockSpec((1, bw), lambda i, g: (0, wcoff + g)),
            pl.BlockSpec((GSS, 1, HEADS_PER_GROUP), grp),
            pl.BlockSpec((GSS, 1, HEADS_PER_GROUP), grp),
            pl.BlockSpec((1, xw), lambda i, g: (0, g)),
        ],
        out_specs=[
            pl.BlockSpec((SQ, xw), lambda i, g: (i, g)),
            pl.BlockSpec((SQ, xw, D_STATE), lambda i, g: (i, g, 0)),
        ],
        out_shape=[
            jax.ShapeDtypeStruct((NS, D_INNER), f32),
            jax.ShapeDtypeStruct((NS, D_INNER, D_STATE), f32),
        ],
        compiler_params=_cparams(("arbitrary", "arbitrary")),
        name="ssd_sample",
    )(proj_s, proj_s, proj_s, state_conv, state_conv, state_conv, proj_s, dtsg,
      state_ssm.reshape(NS, D_INNER, D_STATE), conv_w, conv_w, conv_w, conv_b, conv_b, conv_b,
      a_log.reshape(N_SSM_GROUPS, 1, HEADS_PER_GROUP), d_skip.reshape(N_SSM_GROUPS, 1, HEADS_PER_GROUP),
      ssm_norm.reshape(1, D_INNER))


def _resid_norm_body(mp, ms, xp, xs, w, op, os_):
    i = pl.program_id(0)

    @pl.when(i < NTR)
    def _():
        op[...] = xp[...] + _rms(mp[...].astype(f32), w[...])

    @pl.when(i == NTR)
    def _():
        os_[...] = xs[...] + _rms(ms[...].astype(f32), w[...])


def _resid_norm(name, m_pair, x_pair, w_post):
    pidx = lambda i: (jnp.minimum(i, NTR - 1), 0)
    const = lambda i: (0, 0)
    return pl.pallas_call(
        _resid_norm_body,
        grid=(NTR + 1,),
        in_specs=[
            pl.BlockSpec((TR, D_MODEL), pidx), pl.BlockSpec((NS, D_MODEL), const),
            pl.BlockSpec((TR, D_MODEL), pidx), pl.BlockSpec((NS, D_MODEL), const),
            pl.BlockSpec((1, D_MODEL), const),
        ],
        out_specs=[pl.BlockSpec((TR, D_MODEL), pidx), pl.BlockSpec((NS, D_MODEL), const)],
        out_shape=[jax.ShapeDtypeStruct((NP, D_MODEL), f32), jax.ShapeDtypeStruct((NS, D_MODEL), f32)],
        compiler_params=_cparams(("arbitrary",)),
        name=name,
    )(m_pair[0], m_pair[1], x_pair[0], x_pair[1], w_post.reshape(1, D_MODEL))


def kernel(x_prompt, x_sample, cache_win_k, cache_win_v, state_conv, state_ssm, norm_mix_pre, norm_mix_post,
           w_in, attn_sinks, w_attn_branch, conv_w, conv_b, dt_bias, a_log, d_skip, ssm_norm, w_ssm_branch,
           w_out, norm_ffn_pre, norm_ffn_post, w_gate_up, w_down):
    assert x_prompt.shape == (BATCH, SEQ, D_MODEL) and x_sample.shape == (DEC_BATCH, 1, D_MODEL)
    assert w_in.shape[0] == 1, "one trunk layer"
    xp = x_prompt.reshape(NP, D_MODEL)
    xs = x_sample.reshape(NS, D_MODEL)
    w_in_t = jnp.swapaxes(w_in[0], 0, 1)
    sct = jnp.swapaxes(state_conv[0], 0, 1)

    hn_p, hn_s, dtg, dtt, dtsg = _prenorm(xp, xs, norm_mix_pre[0], w_in_t, dt_bias[0])

    first = lambda rs, rows, cols: [rs[0]]
    (proj,) = _rows_matmul("in_proj", [(hn_p, hn_s)], [(w_in_t, 0, 0, True)], [(0, 0)], first, [(bf16, f32)],
                           tn=1024, n_panels=N_MAIN // 1024)
    proj_p, proj_s = proj

    attn_p, win_k_p = _attn_prompt(proj_p, attn_sinks[0])
    cos_s, sin_s = _rope_tables(jnp.full((1,), PAST_LEN, dtype=jnp.int32))
    attn_s, win_k_s, win_v_s = _attn_sample(proj_s, cache_win_k[0], cache_win_v[0], cos_s, sin_s, attn_sinks[0])

    y_p, ssm_p = _ssd_prompt(proj_p, dtg, dtt, conv_w[0], conv_b[0], a_log[0], d_skip[0], ssm_norm[0])
    y_s, ssm_s = _ssd_sample(proj_s, dtsg, sct, state_ssm[0], conv_w[0], conv_b[0], a_log[0],
                             d_skip[0], ssm_norm[0])

    (gates,) = _rows_matmul("gates", [(hn_p, hn_s)], [(w_in_t, OFF_DT // 1024, OFF_GATES - OFF_DT, True)], [(0, 0)],
                            lambda rs, rows, cols: [_sigmoid(rs[0])], [(bf16, f32)],
                            tn=1024, n_panels=2 * D_MODEL // 1024)
    (ssm_d,) = _rows_matmul("ssm_branch", [(y_p, y_s)], [(w_ssm_branch[0], 0, 0, False)], [(0, 0)],
                            lambda rs, rows, cols: [rows[0] * rs[0]], [(bf16, f32)],
                            tn=512, n_panels=D_MODEL // 512, row_pairs=[gates + (D_MODEL // 512,)])
    (merged,) = _rows_matmul("attn_branch_merge", [(attn_p, attn_s)], [(w_attn_branch[0], 0, 0, False)], [(0, 0)],
                             lambda rs, rows, cols: [rows[0] * rs[0] + rows[1]], [(bf16, f32)],
                             tn=1024, n_panels=D_MODEL // 1024, row_pairs=[gates + (0,), ssm_d + (0,)])
    def mix_residual(rs, rows, cols):
        h = rows[0] + _rms(rs[0], cols[0])
        return [h, _rms(h, cols[1])]

    (h_p, h_s), (hn2_p, hn2_s) = _rows_matmul(
        "out_proj_residual", [merged], [(w_out[0], 0, 0, False)], [(0, 0)], mix_residual, [(f32, f32), (bf16, f32)],
        tn=D_MODEL, n_panels=1, row_pairs=[(xp, xs, 0)],
        cols=[norm_mix_post[0].reshape(1, D_MODEL), norm_ffn_pre[0].reshape(1, D_MODEL)],
        tp=256, single_buffer_weights=True)

    (act,) = _rows_matmul("ffn_gate_up", [(hn2_p, hn2_s)],
                          [(w_gate_up[0], 0, 0, False), (w_gate_up[0], D_FF // 512, 0, False)],
                          [(0, 0), (0, 1)], lambda rs, rows, cols: [_silu(rs[0]) * rs[1]], [(bf16, f32)],
                          tn=512, n_panels=D_FF // 512)
    (ffn,) = _rows_matmul("ffn_down", [act], [(w_down[0], 0, 0, False)], [(0, 0)], first, [(bf16, f32)],
                          tn=512, n_panels=D_MODEL // 512, single_buffer_weights=True)
    out_p, out_s = _resid_norm("ffn_residual", ffn, (h_p, h_s), norm_ffn_post[0])

    proj_b = proj_p.reshape(BATCH, SEQ, N_MAIN)
    win_v_p = proj_b[:, SEQ - WINDOW:, OFF_V:OFF_Z].astype(f32)
    conv_p = proj_b[:, SEQ - (CONV_W - 1):, OFF_XBC:OFF_DT].astype(f32)
    conv_s = jnp.swapaxes(jnp.concatenate([sct[1:], proj_s[None, :, OFF_XBC:OFF_DT]], axis=0), 0, 1)
    return (out_p.reshape(BATCH, SEQ, D_MODEL),
            out_s.reshape(DEC_BATCH, 1, D_MODEL),
            win_k_p.reshape(1, BATCH, WINDOW, N_KV_HEADS, HEAD_DIM),
            win_v_p.reshape(1, BATCH, WINDOW, N_KV_HEADS, HEAD_DIM),
            conv_p.reshape(1, BATCH, CONV_W - 1, CONV_DIM),
            ssm_p.reshape(1, BATCH, N_SSM_HEADS, SSM_HEAD_DIM, D_STATE),
            win_k_s[None],
            win_v_s[None],
            conv_s.reshape(1, NS, CONV_W - 1, CONV_DIM),
            ssm_s.reshape(1, NS, N_SSM_HEADS, SSM_HEAD_DIM, D_STATE))
```
